```python
import math
import jax, jax.numpy as jnp
from jax import lax
import numpy as np

D_MODEL = 2048
BATCH = 8
SEQ = 2048
DEPTH = 1

GRID_W = 64
NA_HEADS = 8
NA_HEAD_DIM = 128
NA_WIN_ROWS = 8
NA_WIN_COLS = 16
NA_WIDTH = NA_HEADS * NA_HEAD_DIM
MLA_HEADS = 8
MLA_Q_RANK = 512
MLA_KV_RANK = 512
MLA_NOPE_DIM = 128
MLA_ROPE_DIM = 64
MLA_V_DIM = 128
MLA_QK_DIM = MLA_NOPE_DIM + MLA_ROPE_DIM
MLA_WIDTH = MLA_HEADS * MLA_V_DIM
ROPE_THETA = 10000.0
Q_BLOCK = 128
D_FF = 5632
PL_DIM = 256
NORM_EPS = 1e-6
NEG_INF = -1e30
IN_SIZES = (NA_WIDTH, NA_WIDTH, NA_WIDTH, MLA_Q_RANK, MLA_KV_RANK, MLA_ROPE_DIM, D_MODEL, D_MODEL)
N_IN = NA_WIDTH * 3 + MLA_Q_RANK + MLA_KV_RANK + MLA_ROPE_DIM + 2 * D_MODEL

kernel_name = "hybrid_na2d_mla_macaron_encoder"


def rmsnorm(x, g):
    xf = x.astype(jnp.float32)
    y = xf * lax.rsqrt(jnp.mean(xf * xf, axis=-1, keepdims=True) + NORM_EPS)
    return (y * g.astype(jnp.float32)).astype(x.dtype)


def swiglu(x, w_gate, w_up, w_down):
    return (jax.nn.silu(x @ w_gate) * (x @ w_up)) @ w_down


def split_points():
    pts, acc = [], 0
    for s in IN_SIZES[:-1]:
        acc += s
        pts.append(acc)
    return pts


def rope_tables(seq_len, dim):
    pos = jnp.arange(seq_len, dtype=jnp.float32)
    inv_freq = 1.0 / (ROPE_THETA ** (jnp.arange(0, dim, 2, dtype=jnp.float32) / dim))
    ang = pos[:, None] * inv_freq[None, :]
    return jnp.cos(ang), jnp.sin(ang)


def apply_rope(x, cos, sin):
    half = x.shape[-1] // 2
    x1, x2 = x[..., :half], x[..., half:]
    cos = cos.astype(x.dtype)
    sin = sin.astype(x.dtype)
    return jnp.concatenate([x1 * cos - x2 * sin, x2 * cos + x1 * sin], axis=-1)


def neighbourhood_attention(q, k, v, rpb):
    B, S, _ = q.shape
    rows = S // GRID_W
    kh = min(NA_WIN_ROWS, rows)
    kw = NA_WIN_COLS
    grid = lambda t: t.reshape(B, rows, GRID_W, NA_HEADS, NA_HEAD_DIM)
    qg, kg, vg = grid(q), grid(k), grid(v)
    cols = np.arange(GRID_W)
    col_start = np.clip(cols - kw // 2, 0, GRID_W - kw)
    col_mask = (cols[None, :] >= col_start[:, None]) & (cols[None, :] < col_start[:, None] + kw)
    dc_idx = np.clip(cols[None, :] - cols[:, None], -(kw - 1), kw - 1) + (kw - 1)
    col_mask = jnp.asarray(col_mask)[:, None, :]
    dc_idx = jnp.asarray(dc_idx)
    scale = NA_HEAD_DIM ** -0.5

    def row_block(args):
        q_row, r = args
        rs = jnp.clip(r - kh // 2, 0, rows - kh)
        k_rows = lax.dynamic_slice_in_dim(kg, rs, kh, axis=1)
        v_rows = lax.dynamic_slice_in_dim(vg, rs, kh, axis=1)
        dr = rs + jnp.arange(kh) - r
        bias = rpb[:, dr + (NA_WIN_ROWS - 1)][:, :, dc_idx]
        bias = bias.transpose(0, 2, 1, 3).astype(jnp.float32)
        s = jnp.einsum('bqhd,bikhd->bhqik', q_row, k_rows).astype(jnp.float32) * scale + bias
        s = jnp.where(col_mask, s, NEG_INF)
        pr = jax.nn.softmax(s.reshape(B, NA_HEADS, GRID_W, kh * GRID_W), axis=-1)
        pr = pr.reshape(s.shape).astype(v.dtype)
        return jnp.einsum('bhqik,bikhd->bqhd', pr, v_rows)

    o = lax.map(row_block, (qg.swapaxes(0, 1), jnp.arange(rows)))
    return o.swapaxes(0, 1).reshape(B, S, NA_WIDTH)


def mla_attention(q_lat, kv_lat, k_rope_in, q_a_norm, w_uq, kv_a_norm, w_ukv):
    B, S, _ = q_lat.shape
    cq = rmsnorm(q_lat, q_a_norm)
    q = (cq @ w_uq).reshape(B, S, MLA_HEADS, MLA_QK_DIM)
    q_nope, q_rope = q[..., :MLA_NOPE_DIM], q[..., MLA_NOPE_DIM:]
    ckv = rmsnorm(kv_lat, kv_a_norm)
    kv = (ckv @ w_ukv).reshape(B, S, MLA_HEADS, MLA_NOPE_DIM + MLA_V_DIM)
    k_nope, v = kv[..., :MLA_NOPE_DIM], kv[..., MLA_NOPE_DIM:]
    cos, sin = rope_tables(S, MLA_ROPE_DIM)
    q_rope = apply_rope(q_rope, cos[:, None, :], sin[:, None, :])
    k_rope = apply_rope(k_rope_in, cos, sin)
    scale = MLA_QK_DIM ** -0.5
    nb = S // Q_BLOCK
    to_blocks = lambda t: t.reshape(B, nb, Q_BLOCK, *t.shape[2:]).swapaxes(0, 1)

    def q_block(args):
        qn, qr = args
        s = jnp.einsum('bqhd,bkhd->bhqk', qn, k_nope) + jnp.einsum('bqhr,bkr->bhqk', qr, k_rope)
        pr = jax.nn.softmax(s.astype(jnp.float32) * scale, axis=-1).astype(v.dtype)
        return jnp.einsum('bhqk,bkhd->bqhd', pr, v)

    o = lax.map(q_block, (to_blocks(q_nope), to_blocks(q_rope)))
    return o.swapaxes(0, 1).reshape(B, S, MLA_WIDTH)


def setup_inputs(seed: int = 0) -> dict:
    key = jax.random.key(seed)
    ks = iter(jax.random.split(key, 32))
    f32 = jnp.float32
    w = lambda shape, fan_in: jax.random.normal(next(ks), shape, f32) * (fan_in ** -0.5)
    gain = lambda shape: 1.0 + 0.01 * jax.random.normal(next(ks), shape, f32)
    L = DEPTH
    return {
        "x": jax.random.normal(next(ks), (BATCH, SEQ, D_MODEL), f32),
        "p": jax.random.normal(next(ks), (DEPTH, BATCH, SEQ, PL_DIM), f32),
        "ffn1_norm": gain((L, D_MODEL)),
        "ffn1_w_gate": w((L, D_MODEL, D_FF), D_MODEL),
        "ffn1_w_up": w((L, D_MODEL, D_FF), D_MODEL),
        "ffn1_w_down": w((L, D_FF, D_MODEL), D_FF),
        "mix_norm": gain((L, D_MODEL)),
        "w_in": w((L, D_MODEL, N_IN), D_MODEL),
        "q_a_norm": gain((L, MLA_Q_RANK)),
        "w_uq": w((L, MLA_Q_RANK, MLA_HEADS * MLA_QK_DIM), MLA_Q_RANK),
        "kv_a_norm": gain((L, MLA_KV_RANK)),
        "w_ukv": w((L, MLA_KV_RANK, MLA_HEADS * (MLA_NOPE_DIM + MLA_V_DIM)), MLA_KV_RANK),
        "na_rpb": 0.02 * jax.random.normal(next(ks), (L, NA_HEADS, 2 * NA_WIN_ROWS - 1, 2 * NA_WIN_COLS - 1), f32),
        "w_branch_a": w((L, NA_WIDTH, D_MODEL), NA_WIDTH),
        "w_branch_b": w((L, MLA_WIDTH, D_MODEL), MLA_WIDTH),
        "w_out": w((L, D_MODEL, D_MODEL), D_MODEL),
        "ffn2_norm": gain((L, D_MODEL)),
        "ffn2_w_gate": w((L, D_MODEL, D_FF), D_MODEL),
        "ffn2_w_up": w((L, D_MODEL, D_FF), D_MODEL),
        "ffn2_w_down": w((L, D_FF, D_MODEL), D_FF),
        "pl_norm": gain((L, D_MODEL)),
        "w_pl": w((L, PL_DIM, D_MODEL), PL_DIM),
        "w_pl_gate": w((L, D_MODEL, D_MODEL), D_MODEL),
        "final_norm": gain((D_MODEL,)),
    }


def reference(x, p, ffn1_norm, ffn1_w_gate, ffn1_w_up, ffn1_w_down, mix_norm, w_in,
              q_a_norm, w_uq, kv_a_norm, w_ukv, na_rpb, w_branch_a, w_branch_b, w_out,
              ffn2_norm, ffn2_w_gate, ffn2_w_up, ffn2_w_down, pl_norm, w_pl, w_pl_gate,
              final_norm):
    pts = split_points()
    h = x
    for i in range(DEPTH):
        h = h + 0.5 * swiglu(rmsnorm(h, ffn1_norm[i]), ffn1_w_gate[i], ffn1_w_up[i], ffn1_w_down[i])
        u = rmsnorm(h, mix_norm[i])
        z = u @ w_in[i]
        na_q, na_k, na_v, q_lat, kv_lat, k_rope, gate_a, gate_b = jnp.split(z, pts, axis=-1)
        y_a = neighbourhood_attention(na_q, na_k, na_v, na_rpb[i]) @ w_branch_a[i]
        y_b = mla_attention(q_lat, kv_lat, k_rope, q_a_norm[i], w_uq[i], kv_a_norm[i], w_ukv[i]) @ w_branch_b[i]
        merged = jax.nn.sigmoid(gate_a) * y_a + jax.nn.sigmoid(gate_b) * y_b
        h = h + merged @ w_out[i]
        h = h + 0.5 * swiglu(rmsnorm(h, ffn2_norm[i]), ffn2_w_gate[i], ffn2_w_up[i], ffn2_w_down[i])
        pl_gate = jax.nn.sigmoid(rmsnorm(h, pl_norm[i]) @ w_pl_gate[i])
        h = h + pl_gate * (p[i] @ w_pl[i])
    return rmsnorm(h, final_norm)
```

```python
import functools
import math

import jax
import jax.numpy as jnp
import numpy as np
from jax import lax
from jax.experimental import pallas as pl
from jax.experimental.pallas import tpu as pltpu

F32 = jnp.float32
BF16 = jnp.bfloat16

GRID_W = 64
NA_HEADS = 8
NA_HEAD_DIM = 128
NA_WIN_ROWS = 8
NA_WIN_COLS = 16
NA_WIDTH = NA_HEADS * NA_HEAD_DIM
MLA_HEADS = 8
MLA_Q_RANK = 512
MLA_KV_RANK = 512
MLA_NOPE_DIM = 128
MLA_ROPE_DIM = 64
MLA_V_DIM = 128
MLA_QK_DIM = MLA_NOPE_DIM + MLA_ROPE_DIM
MLA_WIDTH = MLA_HEADS * MLA_V_DIM
ROPE_THETA = 10000.0
NORM_EPS = 1e-6
NEG_INF = -1e30

LANES = 128
MLA_HEAD_PAD = 2 * LANES
VMEM_LIMIT_BYTES = 56 * 1024 * 1024


def _cparams(sem):
    return pltpu.CompilerParams(dimension_semantics=sem, vmem_limit_bytes=VMEM_LIMIT_BYTES)


def _resident(shape):
    return pl.BlockSpec(shape, lambda *_: (0, 0), pipeline_mode=pl.Buffered(1))


def _rms(x, g):
    ms = jnp.mean(x * x, axis=-1, keepdims=True)
    return x * lax.rsqrt(ms + NORM_EPS) * g


def _sigmoid(x):
    return 1.0 / (1.0 + jnp.exp(-x))


def _ffn_body(x_ref, nw_ref, wg_ref, wu_ref, wd_ref, o_ref, xn_ref):
    j = pl.program_id(1)

    @pl.when(j == 0)
    def _():
        x = x_ref[...]
        xn_ref[...] = _rms(x, nw_ref[...]).astype(BF16)
        o_ref[...] = x

    xn = xn_ref[...]
    g = jnp.dot(xn, wg_ref[...], preferred_element_type=F32)
    u = jnp.dot(xn, wu_ref[...], preferred_element_type=F32)
    a = (g * _sigmoid(g) * u * 0.5).astype(BF16)
    o_ref[...] += jnp.dot(a, wd_ref[...], preferred_element_type=F32)


def _ffn(x, norm_w, wg, wu, wd, *, tm=512, tf=512):
    t, d = x.shape
    f = wg.shape[1]
    return pl.pallas_call(
        _ffn_body,
        grid=(t // tm, f // tf),
        in_specs=[
            pl.BlockSpec((tm, d), lambda i, j: (i, 0)),
            pl.BlockSpec((1, d), lambda i, j: (0, 0)),
            pl.BlockSpec((d, tf), lambda i, j: (0, j)),
            pl.BlockSpec((d, tf), lambda i, j: (0, j)),
            pl.BlockSpec((tf, d), lambda i, j: (j, 0)),
        ],
        out_specs=pl.BlockSpec((tm, d), lambda i, j: (i, 0)),
        out_shape=jax.ShapeDtypeStruct((t, d), F32),
        scratch_shapes=[pltpu.VMEM((tm, d), BF16)],
        compiler_params=_cparams(("parallel", "arbitrary")),
        name="ffn",
    )(x, norm_w, wg, wu, wd)


def _in_proj_body(h_ref, nw_ref, w_ref, wkr_ref, z_ref, kr_ref, xn_ref):
    j = pl.program_id(1)

    @pl.when(j == 0)
    def _():
        xn = _rms(h_ref[...], nw_ref[...]).astype(BF16)
        xn_ref[...] = xn
        kr_ref[...] = jnp.dot(xn, wkr_ref[...], preferred_element_type=F32)

    z_ref[...] = jnp.dot(xn_ref[...], w_ref[...], preferred_element_type=F32).astype(BF16)


def _in_proj(h, norm_w, w_main, w_kr, *, tm=1024, tn=1024):
    t, d = h.shape
    n = w_main.shape[1]
    return pl.pallas_call(
        _in_proj_body,
        grid=(t // tm, n // tn),
        in_specs=[
            pl.BlockSpec((tm, d), lambda i, j: (i, 0)),
            pl.BlockSpec((1, d), lambda i, j: (0, 0)),
            pl.BlockSpec((d, tn), lambda i, j: (0, j)),
            pl.BlockSpec((d, LANES), lambda i, j: (0, 0)),
        ],
        out_specs=[
            pl.BlockSpec((tm, tn), lambda i, j: (i, j)),
            pl.BlockSpec((tm, LANES), lambda i, j: (i, 0)),
        ],
        out_shape=[
            jax.ShapeDtypeStruct((t, n), BF16),
            jax.ShapeDtypeStruct((t, LANES), F32),
        ],
        scratch_shapes=[pltpu.VMEM((tm, d), BF16)],
        compiler_params=_cparams(("parallel", "arbitrary")),
        name="in_proj",
    )(h, norm_w, w_main, w_kr)


def _rope(r, cos_t, sin_t):
    lane = lax.broadcasted_iota(jnp.int32, r.shape, 1)
    half = MLA_ROPE_DIM // 2
    swapped = jnp.where(lane < half, pltpu.roll(r, LANES - half, 1), pltpu.roll(r, half, 1))
    return r * cos_t + swapped * sin_t


def _mla_prep_body(ql_ref, kvl_ref, kr_ref, qn_ref, kvn_ref, wuq_ref, wukv_ref,
                   cos_ref, sin_ref, q_out, k_out, v_out):
    cq = _rms(ql_ref[...].astype(F32), qn_ref[...]).astype(BF16)
    ckv = _rms(kvl_ref[...].astype(F32), kvn_ref[...]).astype(BF16)
    cos_t = cos_ref[...]
    sin_t = sin_ref[...]
    k_rope = _rope(kr_ref[...], cos_t, sin_t).astype(BF16)
    for h in range(MLA_HEADS):
        lo = h * MLA_HEAD_PAD
        mid = lo + LANES
        hi = lo + MLA_HEAD_PAD
        q = jnp.dot(cq, wuq_ref[:, lo:hi], preferred_element_type=F32)
        kv = jnp.dot(ckv, wukv_ref[:, lo:hi], preferred_element_type=F32)
        q_out[:, lo:mid] = q[:, :LANES].astype(BF16)
        q_out[:, mid:hi] = _rope(q[:, LANES:], cos_t, sin_t).astype(BF16)
        k_out[:, lo:mid] = kv[:, :LANES].astype(BF16)
        k_out[:, mid:hi] = k_rope
        v_out[:, h * MLA_V_DIM:(h + 1) * MLA_V_DIM] = kv[:, LANES:].astype(BF16)


def _mla_prep(z, kr, q_norm, kv_norm, wuq_p, wukv, cos_t, sin_t, *, seq, ql_blk, kvl_blk, tm=512):
    t = z.shape[0]
    nseq = seq // tm
    qw = MLA_HEADS * MLA_HEAD_PAD
    return pl.pallas_call(
        _mla_prep_body,
        grid=(t // tm,),
        in_specs=[
            pl.BlockSpec((tm, MLA_Q_RANK), lambda i: (i, ql_blk)),
            pl.BlockSpec((tm, MLA_KV_RANK), lambda i: (i, kvl_blk)),
            pl.BlockSpec((tm, LANES), lambda i: (i, 0)),
            pl.BlockSpec((1, MLA_Q_RANK), lambda i: (0, 0)),
            pl.BlockSpec((1, MLA_KV_RANK), lambda i: (0, 0)),
            pl.BlockSpec((MLA_Q_RANK, qw), lambda i: (0, 0)),
            pl.BlockSpec((MLA_KV_RANK, qw), lambda i: (0, 0)),
            pl.BlockSpec((tm, LANES), lambda i: (i % nseq, 0)),
            pl.BlockSpec((tm, LANES), lambda i: (i % nseq, 0)),
        ],
        out_specs=[
            pl.BlockSpec((tm, qw), lambda i: (i, 0)),
            pl.BlockSpec((tm, qw), lambda i: (i, 0)),
            pl.BlockSpec((tm, MLA_WIDTH), lambda i: (i, 0)),
        ],
        out_shape=[
            jax.ShapeDtypeStruct((t, qw), BF16),
            jax.ShapeDtypeStruct((t, qw), BF16),
            jax.ShapeDtypeStruct((t, MLA_WIDTH), BF16),
        ],
        compiler_params=_cparams(("parallel",)),
        name="mla_prep",
    )(z, z, kr, q_norm, kv_norm, wuq_p, wukv, cos_t, sin_t)


N_ROW_OFFSETS = NA_WIN_ROWS
RPB_ROWS = 2 * NA_WIN_ROWS - 1
RPB_COLS = 2 * NA_WIN_COLS - 1


def _na_bias_body(rpb_ref, o_ref):
    var = pl.program_id(0)
    kw = NA_WIN_COLS
    shape = (GRID_W, LANES)
    q = lax.broadcasted_iota(jnp.int32, shape, 0)
    lane = lax.broadcasted_iota(jnp.int32, shape, 1)
    k = jnp.where(lane < GRID_W, lane, lane - GRID_W)
    first = lane < GRID_W
    col_start = jnp.clip(q - kw // 2, 0, GRID_W - kw)
    valid = (k >= col_start) & (k < col_start + kw)
    dc = jnp.clip(k - q, -(kw - 1), kw - 1) + (kw - 1)

    def per_head(h, carry):
        for pair in range(NA_WIN_ROWS // 2):
            row_a = (2 * pair) - var + (NA_WIN_ROWS - 1)
            base_a = (h * RPB_ROWS + row_a) * RPB_COLS
            base_b = base_a + RPB_COLS
            acc = jnp.zeros(shape, F32)
            for j in range(RPB_COLS):
                val = jnp.where(first, rpb_ref[base_a + j], rpb_ref[base_b + j])
                acc = jnp.where(dc == j, val, acc)
            o_ref[0, h, :, pair * LANES:(pair + 1) * LANES] = jnp.where(valid, acc, NEG_INF)
        return carry

    lax.fori_loop(0, NA_HEADS, per_head, 0)


def _na_bias(rpb_flat):
    width = NA_WIN_ROWS * GRID_W
    return pl.pallas_call(
        _na_bias_body,
        grid=(N_ROW_OFFSETS,),
        in_specs=[pl.BlockSpec(memory_space=pltpu.SMEM)],
        out_specs=pl.BlockSpec((1, NA_HEADS, GRID_W, width), lambda v: (v, 0, 0, 0)),
        out_shape=jax.ShapeDtypeStruct((N_ROW_OFFSETS, NA_HEADS, GRID_W, width), F32),
        compiler_params=_cparams(("arbitrary",)),
        name="na_bias",
    )(rpb_flat)


def _na_window_start(r, rows):
    return jnp.clip(r - NA_WIN_ROWS // 2, 0, rows - NA_WIN_ROWS)


def _na_body(q_ref, k_ref, v_ref, b_ref, o_ref, *, rows):
    r = pl.program_id(1)
    start = pl.multiple_of(_na_window_start(r, rows) * GRID_W, GRID_W)
    nkeys = NA_WIN_ROWS * GRID_W
    scale = NA_HEAD_DIM ** -0.5
    for h in range(NA_HEADS):
        cs = slice(h * NA_HEAD_DIM, (h + 1) * NA_HEAD_DIM)
        q = q_ref[0, :, cs]
        k = k_ref[0, pl.ds(start, nkeys), cs]
        v = v_ref[0, pl.ds(start, nkeys), cs]
        s = lax.dot_general(q, k, (((1,), (1,)), ((), ())), preferred_element_type=F32)
        b = b_ref[0, h]
        s = jnp.where(b > 0.5 * NEG_INF, s * scale + b, NEG_INF)
        m = jnp.max(s, axis=-1, keepdims=True)
        p = jnp.exp(s - m)
        l = jnp.sum(p, axis=-1, keepdims=True)
        o = jnp.dot(p.astype(BF16), v, preferred_element_type=F32) / l
        o_ref[0, :, cs] = o.astype(BF16)


def _na_attn(z3, bias, *, q_blk, k_blk, v_blk):
    b, s, _ = z3.shape
    rows = s // GRID_W
    nkeys = NA_WIN_ROWS * GRID_W

    def bias_idx(bi, r):
        return (r - _na_window_start(r, rows), 0, 0, 0)

    return pl.pallas_call(
        functools.partial(_na_body, rows=rows),
        grid=(b, rows),
        in_specs=[
            pl.BlockSpec((1, GRID_W, NA_WIDTH), lambda bi, r: (bi, r, q_blk)),
            pl.BlockSpec((1, s, NA_WIDTH), lambda bi, r: (bi, 0, k_blk)),
            pl.BlockSpec((1, s, NA_WIDTH), lambda bi, r: (bi, 0, v_blk)),
            pl.BlockSpec((1, NA_HEADS, GRID_W, nkeys), bias_idx),
        ],
        out_specs=pl.BlockSpec((1, GRID_W, NA_WIDTH), lambda bi, r: (bi, r, 0)),
        out_shape=jax.ShapeDtypeStruct((b, s, NA_WIDTH), BF16),
        compiler_params=_cparams(("parallel", "arbitrary")),
        name="na_attn",
    )(z3, z3, z3, bias)


def _mla_body(q_ref, k_ref, v_ref, o_ref):
    scale = MLA_QK_DIM ** -0.5
    s = lax.dot_general(q_ref[0], k_ref[0], (((1,), (1,)), ((), ())),
                        preferred_element_type=F32) * scale
    m = jnp.max(s, axis=-1, keepdims=True)
    p = jnp.exp(s - m)
    l = jnp.sum(p, axis=-1, keepdims=True)
    o = jnp.dot(p.astype(BF16), v_ref[0], preferred_element_type=F32) / l
    o_ref[0] = o.astype(BF16)


def _mla_attn(q3, k3, v3, *, tq=512):
    b, s, _ = q3.shape
    return pl.pallas_call(
        _mla_body,
        grid=(b, MLA_HEADS, s // tq),
        in_specs=[
            pl.BlockSpec((1, tq, MLA_HEAD_PAD), lambda bi, h, qi: (bi, qi, h)),
            pl.BlockSpec((1, s, MLA_HEAD_PAD), lambda bi, h, qi: (bi, 0, h)),
            pl.BlockSpec((1, s, MLA_V_DIM), lambda bi, h, qi: (bi, 0, h)),
        ],
        out_specs=pl.BlockSpec((1, tq, MLA_V_DIM), lambda bi, h, qi: (bi, qi, h)),
        out_shape=jax.ShapeDtypeStruct((b, s, MLA_WIDTH), BF16),
        compiler_params=_cparams(("parallel", "parallel", "arbitrary")),
        name="mla_attn",
    )(q3, k3, v3)


def _merge_body(na_ref, mla_ref, ga_ref, gb_ref, h_ref, wa_ref, wb_ref, wo_ref, o_ref):
    ya = jnp.dot(na_ref[...], wa_ref[...], preferred_element_type=F32)
    yb = jnp.dot(mla_ref[...], wb_ref[...], preferred_element_type=F32)
    merged = _sigmoid(ga_ref[...].astype(F32)) * ya + _sigmoid(gb_ref[...].astype(F32)) * yb
    o_ref[...] = h_ref[...] + jnp.dot(merged.astype(BF16), wo_ref[...],
                                      preferred_element_type=F32)


def _merge(na_o, mla_o, z, h, wa, wb, wo, *, ga_blk, gb_blk, tm=256):
    t, d = h.shape
    return pl.pallas_call(
        _merge_body,
        grid=(t // tm,),
        in_specs=[
            pl.BlockSpec((tm, NA_WIDTH), lambda i: (i, 0)),
            pl.BlockSpec((tm, MLA_WIDTH), lambda i: (i, 0)),
            pl.BlockSpec((tm, d), lambda i: (i, ga_blk)),
            pl.BlockSpec((tm, d), lambda i: (i, gb_blk)),
            pl.BlockSpec((tm, d), lambda i: (i, 0)),
            _resident((NA_WIDTH, d)),
            _resident((MLA_WIDTH, d)),
            _resident((d, d)),
        ],
        out_specs=pl.BlockSpec((tm, d), lambda i: (i, 0)),
        out_shape=jax.ShapeDtypeStruct((t, d), F32),
        compiler_params=_cparams(("parallel",)),
        name="merge",
    )(na_o, mla_o, z, z, h, wa, wb, wo)


def _pl_body(h_ref, p_ref, nw_ref, wg_ref, wp_ref, fn_ref, o_ref, *, final):
    h = h_ref[...]
    xn = _rms(h, nw_ref[...]).astype(BF16)
    gate = _sigmoid(jnp.dot(xn, wg_ref[...], preferred_element_type=F32))
    emb = jnp.dot(p_ref[...].astype(BF16), wp_ref[...], preferred_element_type=F32)
    h = h + gate * emb
    o_ref[...] = _rms(h, fn_ref[...]) if final else h


def _pl_embed(h, p, norm_w, w_gate, w_pl, final_norm, *, final, tm=512):
    t, d = h.shape
    pdim = p.shape[1]
    return pl.pallas_call(
        functools.partial(_pl_body, final=final),
        grid=(t // tm,),
        in_specs=[
            pl.BlockSpec((tm, d), lambda i: (i, 0)),
            pl.BlockSpec((tm, pdim), lambda i: (i, 0)),
            _resident((1, d)),
            _resident((d, d)),
            _resident((pdim, d)),
            _resident((1, d)),
        ],
        out_specs=pl.BlockSpec((tm, d), lambda i: (i, 0)),
        out_shape=jax.ShapeDtypeStruct((t, d), F32),
        compiler_params=_cparams(("parallel",)),
        name="pl_final" if final else "pl_embed",
    )(h, p, norm_w, w_gate, w_pl, final_norm)


def _final_norm_body(h_ref, fn_ref, o_ref):
    o_ref[...] = _rms(h_ref[...], fn_ref[...])


def _rope_tables(seq):
    half = MLA_ROPE_DIM // 2
    pos = jnp.arange(seq, dtype=F32)
    inv_freq = 1.0 / (ROPE_THETA ** (jnp.arange(0, MLA_ROPE_DIM, 2, dtype=F32) / MLA_ROPE_DIM))
    ang = pos[:, None] * inv_freq[None, :]
    cos, sin = jnp.cos(ang), jnp.sin(ang)
    zeros = jnp.zeros((seq, LANES - 2 * half), F32)
    return (jnp.concatenate([cos, cos, zeros], axis=-1),
            jnp.concatenate([-sin, sin, zeros], axis=-1))


def kernel(x, p, ffn1_norm, ffn1_w_gate, ffn1_w_up, ffn1_w_down, mix_norm, w_in, q_a_norm, w_uq, kv_a_norm, w_ukv, na_rpb, w_branch_a, w_branch_b, w_out, ffn2_norm, ffn2_w_gate, ffn2_w_up, ffn2_w_down, pl_norm, w_pl, w_pl_gate, final_norm):
    bsz, seq, d = x.shape
    depth = p.shape[0]
    t = bsz * seq
    bf = lambda w: w.astype(BF16)
    row = lambda g: g.reshape(1, -1).astype(F32)

    c_lat = 3 * NA_WIDTH
    c_rope = c_lat + MLA_Q_RANK + MLA_KV_RANK
    c_gate = c_rope + MLA_ROPE_DIM
    cos_t, sin_t = _rope_tables(seq)

    h = x.reshape(t, d)
    for i in range(depth):
        h = _ffn(h, row(ffn1_norm[i]), bf(ffn1_w_gate[i]), bf(ffn1_w_up[i]), bf(ffn1_w_down[i]))

        w_in_i = w_in[i]
        w_main = bf(jnp.concatenate([w_in_i[:, :c_rope], w_in_i[:, c_gate:]], axis=1))
        w_kr = bf(jnp.pad(w_in_i[:, c_rope:c_gate], ((0, 0), (0, LANES - MLA_ROPE_DIM))))
        z, kr = _in_proj(h, row(mix_norm[i]), w_main, w_kr)

        wuq = w_uq[i].reshape(MLA_Q_RANK, MLA_HEADS, MLA_QK_DIM)
        wuq_p = bf(jnp.pad(wuq, ((0, 0), (0, 0), (0, MLA_HEAD_PAD - MLA_QK_DIM)))
                   .reshape(MLA_Q_RANK, MLA_HEADS * MLA_HEAD_PAD))
        q_m, k_m, v_m = _mla_prep(
            z, kr, row(q_a_norm[i]), row(kv_a_norm[i]), wuq_p, bf(w_ukv[i]), cos_t, sin_t,
            seq=seq, ql_blk=c_lat // MLA_Q_RANK, kvl_blk=(c_lat + MLA_Q_RANK) // MLA_KV_RANK)

        bias = _na_bias(na_rpb[i].reshape(-1).astype(F32))
        z3 = z.reshape(bsz, seq, z.shape[1])
        na_o = _na_attn(z3, bias, q_blk=0, k_blk=1, v_blk=2)
        mla_o = _mla_attn(q_m.reshape(bsz, seq, -1), k_m.reshape(bsz, seq, -1),
                          v_m.reshape(bsz, seq, -1))

        gate_col = c_rope
        h = _merge(na_o.reshape(t, NA_WIDTH), mla_o.reshape(t, MLA_WIDTH), z, h,
                   bf(w_branch_a[i]), bf(w_branch_b[i]), bf(w_out[i]),
                   ga_blk=gate_col // d, gb_blk=gate_col // d + 1)

        h = _ffn(h, row(ffn2_norm[i]), bf(ffn2_w_gate[i]), bf(ffn2_w_up[i]), bf(ffn2_w_down[i]))

        h = _pl_embed(h, p[i].reshape(t, -1), row(pl_norm[i]), bf(w_pl_gate[i]), bf(w_pl[i]),
                      row(final_norm), final=(i == depth - 1))

    if depth == 0:
        tm = 512
        h = pl.pallas_call(
            _final_norm_body,
            grid=(t // tm,),
            in_specs=[pl.BlockSpec((tm, d), lambda i: (i, 0)),
                      pl.BlockSpec((1, d), lambda i: (0, 0))],
            out_specs=pl.BlockSpec((tm, d), lambda i: (i, 0)),
            out_shape=jax.ShapeDtypeStruct((t, d), F32),
            compiler_params=_cparams(("parallel",)),
            name="final_norm",
        )(h, row(final_norm))
    return h.reshape(bsz, seq, d)
```

```python
import functools
import math

import jax
import jax.numpy as jnp
from jax import lax
from jax.experimental import pallas as pl
from jax.experimental.pallas import tpu as pltpu

F32 = jnp.float32
BF16 = jnp.bfloat16

GRID_W = 64
NA_HEADS = 8
NA_HEAD_DIM = 128
NA_WIN_ROWS = 8
NA_WIN_COLS = 16
NA_WIDTH = NA_HEADS * NA_HEAD_DIM
MLA_HEADS = 8
MLA_Q_RANK = 512
MLA_KV_RANK = 512
MLA_NOPE_DIM = 128
MLA_ROPE_DIM = 64
MLA_V_DIM = 128
MLA_QK_DIM = MLA_NOPE_DIM + MLA_ROPE_DIM
MLA_WIDTH = MLA_HEADS * MLA_V_DIM
ROPE_THETA = 10000.0
NORM_EPS = 1e-6
NEG_INF = -1e30
LOG2E = math.log2(math.e)

LANES = 128
MLA_HEAD_PAD = 2 * LANES
VMEM_LIMIT_BYTES = 56 * 1024 * 1024


def _cparams(sem):
    return pltpu.CompilerParams(dimension_semantics=sem, vmem_limit_bytes=VMEM_LIMIT_BYTES)


def _resident(shape):
    return pl.BlockSpec(shape, lambda *_: (0, 0), pipeline_mode=pl.Buffered(1))


def _rms(x, g):
    ms = jnp.mean(x * x, axis=-1, keepdims=True)
    return x * lax.rsqrt(ms + NORM_EPS) * g


def _sigmoid(x):
    return 1.0 / (1.0 + jnp.exp(-x))


def _ffn_body(x_ref, nw_ref, wg_ref, wu_ref, wd_ref, o_ref, xn_ref):
    j = pl.program_id(1)

    @pl.when(j == 0)
    def _():
        x = x_ref[...]
        xn_ref[...] = _rms(x, nw_ref[...]).astype(BF16)
        o_ref[...] = x

    xn = xn_ref[...]
    g = jnp.dot(xn, wg_ref[...], preferred_element_type=F32)
    u = jnp.dot(xn, wu_ref[...], preferred_element_type=F32)
    a = (g * _sigmoid(g) * u * 0.5).astype(BF16)
    o_ref[...] += jnp.dot(a, wd_ref[...], preferred_element_type=F32)


def _ffn(x, norm_w, wg, wu, wd, *, tm=512, tf=512):
    t, d = x.shape
    f = wg.shape[1]
    return pl.pallas_call(
        _ffn_body,
        grid=(t // tm, f // tf),
        in_specs=[
            pl.BlockSpec((tm, d), lambda i, j: (i, 0)),
            pl.BlockSpec((1, d), lambda i, j: (0, 0)),
            pl.BlockSpec((d, tf), lambda i, j: (0, j)),
            pl.BlockSpec((d, tf), lambda i, j: (0, j)),
            pl.BlockSpec((tf, d), lambda i, j: (j, 0)),
        ],
        out_specs=pl.BlockSpec((tm, d), lambda i, j: (i, 0)),
        out_shape=jax.ShapeDtypeStruct((t, d), F32),
        scratch_shapes=[pltpu.VMEM((tm, d), BF16)],
        compiler_params=_cparams(("parallel", "arbitrary")),
        name="ffn",
    )(x, norm_w, wg, wu, wd)


def _in_proj_body(h_ref, nw_ref, w_ref, wkr_ref, z_ref, kr_ref, xn_ref):
    j = pl.program_id(1)

    @pl.when(j == 0)
    def _():
        xn = _rms(h_ref[...], nw_ref[...]).astype(BF16)
        xn_ref[...] = xn
        kr_ref[...] = jnp.dot(xn, wkr_ref[...], preferred_element_type=F32)

    z_ref[...] = jnp.dot(xn_ref[...], w_ref[...], preferred_element_type=F32).astype(BF16)


def _in_proj(h, norm_w, w_main, w_kr, *, tm=1024, tn=1024):
    t, d = h.shape
    n = w_main.shape[1]
    return pl.pallas_call(
        _in_proj_body,
        grid=(t // tm, n // tn),
        in_specs=[
            pl.BlockSpec((tm, d), lambda i, j: (i, 0)),
            pl.BlockSpec((1, d), lambda i, j: (0, 0)),
            pl.BlockSpec((d, tn), lambda i, j: (0, j)),
            pl.BlockSpec((d, LANES), lambda i, j: (0, 0)),
        ],
        out_specs=[
            pl.BlockSpec((tm, tn), lambda i, j: (i, j)),
            pl.BlockSpec((tm, LANES), lambda i, j: (i, 0)),
        ],
        out_shape=[
            jax.ShapeDtypeStruct((t, n), BF16),
            jax.ShapeDtypeStruct((t, LANES), F32),
        ],
        scratch_shapes=[pltpu.VMEM((tm, d), BF16)],
        compiler_params=_cparams(("parallel", "arbitrary")),
        name="in_proj",
    )(h, norm_w, w_main, w_kr)


def _rope(r, cos_t, sin_t):
    lane = lax.broadcasted_iota(jnp.int32, r.shape, 1)
    half = MLA_ROPE_DIM // 2
    swapped = jnp.where(lane < half, pltpu.roll(r, LANES - half, 1), pltpu.roll(r, half, 1))
    return r * cos_t + swapped * sin_t


def _mla_prep_body(ql_ref, kvl_ref, kr_ref, qn_ref, kvn_ref, wuq_ref, wukv_ref,
                   cos_ref, sin_ref, q_out, k_out, v_out):
    cq = _rms(ql_ref[...].astype(F32), qn_ref[...]).astype(BF16)
    ckv = _rms(kvl_ref[...].astype(F32), kvn_ref[...]).astype(BF16)
    cos_t = cos_ref[...]
    sin_t = sin_ref[...]
    k_rope = _rope(kr_ref[...], cos_t, sin_t).astype(BF16)
    for h in range(MLA_HEADS):
        lo = h * MLA_HEAD_PAD
        mid = lo + LANES
        hi = lo + MLA_HEAD_PAD
        q = jnp.dot(cq, wuq_ref[:, lo:hi], preferred_element_type=F32)
        kv = jnp.dot(ckv, wukv_ref[:, lo:hi], preferred_element_type=F32)
        q_out[:, lo:mid] = q[:, :LANES].astype(BF16)
        q_out[:, mid:hi] = _rope(q[:, LANES:], cos_t, sin_t).astype(BF16)
        k_out[:, lo:mid] = kv[:, :LANES].astype(BF16)
        k_out[:, mid:hi] = k_rope
        v_out[:, lo:mid] = kv[:, LANES:].astype(BF16)
        v_out[:, mid:hi] = jnp.ones((kv.shape[0], MLA_HEAD_PAD - MLA_V_DIM), BF16)


def _mla_prep(z, kr, q_norm, kv_norm, wuq_p, wukv, cos_t, sin_t, *, seq, ql_blk, kvl_blk, tm=512):
    t = z.shape[0]
    nseq = seq // tm
    qw = MLA_HEADS * MLA_HEAD_PAD
    return pl.pallas_call(
        _mla_prep_body,
        grid=(t // tm,),
        in_specs=[
            pl.BlockSpec((tm, MLA_Q_RANK), lambda i: (i, ql_blk)),
            pl.BlockSpec((tm, MLA_KV_RANK), lambda i: (i, kvl_blk)),
            pl.BlockSpec((tm, LANES), lambda i: (i, 0)),
            pl.BlockSpec((1, MLA_Q_RANK), lambda i: (0, 0)),
            pl.BlockSpec((1, MLA_KV_RANK), lambda i: (0, 0)),
            pl.BlockSpec((MLA_Q_RANK, qw), lambda i: (0, 0)),
            pl.BlockSpec((MLA_KV_RANK, qw), lambda i: (0, 0)),
            pl.BlockSpec((tm, LANES), lambda i: (i % nseq, 0)),
            pl.BlockSpec((tm, LANES), lambda i: (i % nseq, 0)),
        ],
        out_specs=[
            pl.BlockSpec((tm, qw), lambda i: (i, 0)),
            pl.BlockSpec((tm, qw), lambda i: (i, 0)),
            pl.BlockSpec((tm, qw), lambda i: (i, 0)),
        ],
        out_shape=[
            jax.ShapeDtypeStruct((t, qw), BF16),
            jax.ShapeDtypeStruct((t, qw), BF16),
            jax.ShapeDtypeStruct((t, qw), BF16),
        ],
        compiler_params=_cparams(("parallel",)),
        name="mla_prep",
    )(z, z, kr, q_norm, kv_norm, wuq_p, wukv, cos_t, sin_t)


RPB_ROWS = 2 * NA_WIN_ROWS - 1
RPB_COLS = 2 * NA_WIN_COLS - 1
NA_Q_ROWS = 4
NA_K_ROWS = NA_Q_ROWS + NA_WIN_ROWS
NA_VARIANTS = 3


def _na_variant_geometry(var, j):
    lo = (0, j, NA_K_ROWS - NA_WIN_ROWS)[var]
    rho0 = (NA_WIN_ROWS - 1) - j - (NA_WIN_ROWS // 2) * var
    return lo, rho0


def _na_bias_body(rpb_ref, o_ref, pair_ref):
    h = pl.program_id(0)
    kw = NA_WIN_COLS
    shape = (GRID_W, LANES)
    q = lax.broadcasted_iota(jnp.int32, shape, 0)
    lane = lax.broadcasted_iota(jnp.int32, shape, 1)
    first = lane < GRID_W
    k = jnp.where(first, lane, lane - GRID_W)
    col_start = jnp.clip(q - kw // 2, 0, GRID_W - kw)
    valid = (k >= col_start) & (k < col_start + kw)
    dc = jnp.clip(k - q, -(kw - 1), kw - 1) + (kw - 1)
    neg = jnp.full(shape, NEG_INF, F32)

    for e in range(RPB_ROWS + 1):
        base_a = (h * RPB_ROWS + max(e - 1, 0)) * RPB_COLS
        base_b = (h * RPB_ROWS + min(e, RPB_ROWS - 1)) * RPB_COLS
        acc = jnp.zeros(shape, F32)
        for c in range(RPB_COLS):
            val = jnp.where(first, rpb_ref[base_a + c], rpb_ref[base_b + c])
            acc = jnp.where(dc == c, val, acc)
        pair_ref[e] = jnp.where(valid, acc * LOG2E, NEG_INF)

    for var in range(NA_VARIANTS):
        for j in range(NA_Q_ROWS):
            lo, rho0 = _na_variant_geometry(var, j)
            for pair in range(NA_K_ROWS // 2):
                i = 2 * pair
                in_a = lo <= i < lo + NA_WIN_ROWS
                in_b = lo <= i + 1 < lo + NA_WIN_ROWS
                if in_a or in_b:
                    tile = pair_ref[rho0 + i + 1]
                    if not in_a:
                        tile = jnp.where(first, neg, tile)
                    if not in_b:
                        tile = jnp.where(first, tile, neg)
                else:
                    tile = neg
                o_ref[var, 0, j * GRID_W:(j + 1) * GRID_W, pair * LANES:(pair + 1) * LANES] = tile


def _na_bias(rpb_flat):
    nq = NA_Q_ROWS * GRID_W
    nk = NA_K_ROWS * GRID_W
    return pl.pallas_call(
        _na_bias_body,
        grid=(NA_HEADS,),
        in_specs=[pl.BlockSpec(memory_space=pltpu.SMEM)],
        out_specs=pl.BlockSpec((NA_VARIANTS, 1, nq, nk), lambda h: (0, h, 0, 0)),
        out_shape=jax.ShapeDtypeStruct((NA_VARIANTS, NA_HEADS, nq, nk), F32),
        scratch_shapes=[pltpu.VMEM((RPB_ROWS + 1, GRID_W, LANES), F32)],
        compiler_params=_cparams(("arbitrary",)),
        name="na_bias",
    )(rpb_flat)


def _na_key_start(g, rows):
    return jnp.clip(g * NA_Q_ROWS - NA_WIN_ROWS // 2, 0, rows - NA_K_ROWS)


def _na_body(q_ref, k_ref, v_ref, b_ref, o_ref, *, rows):
    g = pl.program_id(1)
    start = pl.multiple_of(_na_key_start(g, rows) * GRID_W, GRID_W)
    nkeys = NA_K_ROWS * GRID_W
    c = (NA_HEAD_DIM ** -0.5) * LOG2E
    ones = jnp.ones((nkeys, NA_HEAD_DIM), BF16)
    for h in range(NA_HEADS):
        cs = slice(h * NA_HEAD_DIM, (h + 1) * NA_HEAD_DIM)
        q = q_ref[0, :, cs]
        k = k_ref[0, pl.ds(start, nkeys), cs]
        v = v_ref[0, pl.ds(start, nkeys), cs]
        s = lax.dot_general(q, k, (((1,), (1,)), ((), ())), preferred_element_type=F32)
        t = s * c + b_ref[0, h]
        m = jnp.max(t, axis=-1, keepdims=True)
        p = jnp.exp2(t - m).astype(BF16)
        oe = jnp.dot(p, jnp.concatenate([v, ones], axis=1), preferred_element_type=F32)
        o_ref[0, :, cs] = (oe[:, :NA_HEAD_DIM] / oe[:, NA_HEAD_DIM:]).astype(BF16)


def _na_attn(z3, bias, *, q_blk, k_blk, v_blk):
    b, s, _ = z3.shape
    rows = s // GRID_W
    assert rows % NA_Q_ROWS == 0 and rows >= 2 * NA_K_ROWS - NA_WIN_ROWS
    groups = rows // NA_Q_ROWS
    nq = NA_Q_ROWS * GRID_W

    def bias_idx(bi, g):
        return (jnp.where(g == 0, 0, jnp.where(g == groups - 1, 2, 1)), 0, 0, 0)

    return pl.pallas_call(
        functools.partial(_na_body, rows=rows),
        grid=(b, groups),
        in_specs=[
            pl.BlockSpec((1, nq, NA_WIDTH), lambda bi, g: (bi, g, q_blk)),
            pl.BlockSpec((1, s, NA_WIDTH), lambda bi, g: (bi, 0, k_blk)),
            pl.BlockSpec((1, s, NA_WIDTH), lambda bi, g: (bi, 0, v_blk)),
            pl.BlockSpec((1, NA_HEADS, nq, NA_K_ROWS * GRID_W), bias_idx),
        ],
        out_specs=pl.BlockSpec((1, nq, NA_WIDTH), lambda bi, g: (bi, g, 0)),
        out_shape=jax.ShapeDtypeStruct((b, s, NA_WIDTH), BF16),
        compiler_params=_cparams(("parallel", "arbitrary")),
        name="na_attn",
    )(z3, z3, z3, bias)


def _mla_body(q_ref, k_ref, v_ref, o_ref, *, tq, tk):
    seq = q_ref.shape[1]
    c = (MLA_QK_DIM ** -0.5) * LOG2E
    for qi in range(seq // tq):
        q = q_ref[0, qi * tq:(qi + 1) * tq, :]
        m = acc = None
        for kj in range(seq // tk):
            ks = slice(kj * tk, (kj + 1) * tk)
            s = lax.dot_general(q, k_ref[0, ks, :], (((1,), (1,)), ((), ())),
                                preferred_element_type=F32) * c
            mj = jnp.max(s, axis=-1, keepdims=True)
            m_new = mj if m is None else jnp.maximum(m, mj)
            p = jnp.exp2(s - m_new).astype(BF16)
            pv = jnp.dot(p, v_ref[0, ks, :], preferred_element_type=F32)
            acc = pv if acc is None else acc * jnp.exp2(m - m_new) + pv
            m = m_new
        o_ref[0, qi * tq:(qi + 1) * tq, :] = (acc[:, :MLA_V_DIM] / acc[:, MLA_V_DIM:]).astype(BF16)


def _mla_attn(q3, k3, v3, *, tq=512, tk=512):
    b, s, _ = q3.shape
    vw = v3.shape[2] // MLA_HEADS
    return pl.pallas_call(
        functools.partial(_mla_body, tq=tq, tk=tk),
        grid=(b, MLA_HEADS),
        in_specs=[
            pl.BlockSpec((1, s, MLA_HEAD_PAD), lambda bi, h: (bi, 0, h)),
            pl.BlockSpec((1, s, MLA_HEAD_PAD), lambda bi, h: (bi, 0, h)),
            pl.BlockSpec((1, s, vw), lambda bi, h: (bi, 0, h)),
        ],
        out_specs=pl.BlockSpec((1, s, MLA_V_DIM), lambda bi, h: (bi, 0, h)),
        out_shape=jax.ShapeDtypeStruct((b, s, MLA_WIDTH), BF16),
        compiler_params=_cparams(("parallel", "arbitrary")),
        name="mla_attn",
    )(q3, k3, v3)


def _merge_body(na_ref, mla_ref, ga_ref, gb_ref, h_ref, wa_ref, wb_ref, wo_ref, o_ref):
    ya = jnp.dot(na_ref[...], wa_ref[...], preferred_element_type=F32)
    yb = jnp.dot(mla_ref[...], wb_ref[...], preferred_element_type=F32)
    merged = _sigmoid(ga_ref[...].astype(F32)) * ya + _sigmoid(gb_ref[...].astype(F32)) * yb
    o_ref[...] = h_ref[...] + jnp.dot(merged.astype(BF16), wo_ref[...],
                                      preferred_element_type=F32)


def _merge(na_o, mla_o, z, h, wa, wb, wo, *, ga_blk, gb_blk, tm=256):
    t, d = h.shape
    return pl.pallas_call(
        _merge_body,
        grid=(t // tm,),
        in_specs=[
            pl.BlockSpec((tm, NA_WIDTH), lambda i: (i, 0)),
            pl.BlockSpec((tm, MLA_WIDTH), lambda i: (i, 0)),
            pl.BlockSpec((tm, d), lambda i: (i, ga_blk)),
            pl.BlockSpec((tm, d), lambda i: (i, gb_blk)),
            pl.BlockSpec((tm, d), lambda i: (i, 0)),
            _resident((NA_WIDTH, d)),
            _resident((MLA_WIDTH, d)),
            _resident((d, d)),
        ],
        out_specs=pl.BlockSpec((tm, d), lambda i: (i, 0)),
        out_shape=jax.ShapeDtypeStruct((t, d), F32),
        compiler_params=_cparams(("parallel",)),
        name="merge",
    )(na_o, mla_o, z, z, h, wa, wb, wo)


def _pl_body(h_ref, p_ref, nw_ref, wg_ref, wp_ref, fn_ref, o_ref, *, final):
    h = h_ref[...]
    xn = _rms(h, nw_ref[...]).astype(BF16)
    gate = _sigmoid(jnp.dot(xn, wg_ref[...], preferred_element_type=F32))
    emb = jnp.dot(p_ref[...].astype(BF16), wp_ref[...], preferred_element_type=F32)
    h = h + gate * emb
    o_ref[...] = _rms(h, fn_ref[...]) if final else h


def _pl_embed(h, p, norm_w, w_gate, w_pl, final_norm, *, final, tm=512):
    t, d = h.shape
    pdim = p.shape[1]
    return pl.pallas_call(
        functools.partial(_pl_body, final=final),
        grid=(t // tm,),
        in_specs=[
            pl.BlockSpec((tm, d), lambda i: (i, 0)),
            pl.BlockSpec((tm, pdim), lambda i: (i, 0)),
            _resident((1, d)),
            _resident((d, d)),
            _resident((pdim, d)),
            _resident((1, d)),
        ],
        out_specs=pl.BlockSpec((tm, d), lambda i: (i, 0)),
        out_shape=jax.ShapeDtypeStruct((t, d), F32),
        compiler_params=_cparams(("parallel",)),
        name="pl_final" if final else "pl_embed",
    )(h, p, norm_w, w_gate, w_pl, final_norm)


def _final_norm_body(h_ref, fn_ref, o_ref):
    o_ref[...] = _rms(h_ref[...], fn_ref[...])


def _rope_tables(seq):
    half = MLA_ROPE_DIM // 2
    pos = jnp.arange(seq, dtype=F32)
    inv_freq = 1.0 / (ROPE_THETA ** (jnp.arange(0, MLA_ROPE_DIM, 2, dtype=F32) / MLA_ROPE_DIM))
    ang = pos[:, None] * inv_freq[None, :]
    cos, sin = jnp.cos(ang), jnp.sin(ang)
    zeros = jnp.zeros((seq, LANES - 2 * half), F32)
    return (jnp.concatenate([cos, cos, zeros], axis=-1),
            jnp.concatenate([-sin, sin, zeros], axis=-1))


def kernel(x, p, ffn1_norm, ffn1_w_gate, ffn1_w_up, ffn1_w_down, mix_norm, w_in, q_a_norm, w_uq, kv_a_norm, w_ukv, na_rpb, w_branch_a, w_branch_b, w_out, ffn2_norm, ffn2_w_gate, ffn2_w_up, ffn2_w_down, pl_norm, w_pl, w_pl_gate, final_norm):
    bsz, seq, d = x.shape
    depth = p.shape[0]
    t = bsz * seq
    bf = lambda w: w.astype(BF16)
    row = lambda g: g.reshape(1, -1).astype(F32)

    c_lat = 3 * NA_WIDTH
    c_rope = c_lat + MLA_Q_RANK + MLA_KV_RANK
    c_gate = c_rope + MLA_ROPE_DIM
    cos_t, sin_t = _rope_tables(seq)

    h = x.reshape(t, d)
    for i in range(depth):
        h = _ffn(h, row(ffn1_norm[i]), bf(ffn1_w_gate[i]), bf(ffn1_w_up[i]), bf(ffn1_w_down[i]))

        w_in_i = w_in[i]
        w_main = bf(jnp.concatenate([w_in_i[:, :c_rope], w_in_i[:, c_gate:]], axis=1))
        w_kr = bf(jnp.pad(w_in_i[:, c_rope:c_gate], ((0, 0), (0, LANES - MLA_ROPE_DIM))))
        z, kr = _in_proj(h, row(mix_norm[i]), w_main, w_kr)

        wuq = w_uq[i].reshape(MLA_Q_RANK, MLA_HEADS, MLA_QK_DIM)
        wuq_p = bf(jnp.pad(wuq, ((0, 0), (0, 0), (0, MLA_HEAD_PAD - MLA_QK_DIM)))
                   .reshape(MLA_Q_RANK, MLA_HEADS * MLA_HEAD_PAD))
        q_m, k_m, v_m = _mla_prep(
            z, kr, row(q_a_norm[i]), row(kv_a_norm[i]), wuq_p, bf(w_ukv[i]), cos_t, sin_t,
            seq=seq, ql_blk=c_lat // MLA_Q_RANK, kvl_blk=(c_lat + MLA_Q_RANK) // MLA_KV_RANK)

        bias = _na_bias(na_rpb[i].reshape(-1).astype(F32))
        z3 = z.reshape(bsz, seq, z.shape[1])
        na_o = _na_attn(z3, bias, q_blk=0, k_blk=1, v_blk=2)
        mla_o = _mla_attn(q_m.reshape(bsz, seq, -1), k_m.reshape(bsz, seq, -1),
                          v_m.reshape(bsz, seq, -1))

        gate_col = c_rope
        h = _merge(na_o.reshape(t, NA_WIDTH), mla_o.reshape(t, MLA_WIDTH), z, h,
                   bf(w_branch_a[i]), bf(w_branch_b[i]), bf(w_out[i]),
                   ga_blk=gate_col // d, gb_blk=gate_col // d + 1)

        h = _ffn(h, row(ffn2_norm[i]), bf(ffn2_w_gate[i]), bf(ffn2_w_up[i]), bf(ffn2_w_down[i]))

        h = _pl_embed(h, p[i].reshape(t, -1), row(pl_norm[i]), bf(w_pl_gate[i]), bf(w_pl[i]),
                      row(final_norm), final=(i == depth - 1))

    if depth == 0:
        tm = 512
        h = pl.pallas_call(
            _final_norm_body,
            grid=(t // tm,),
            in_specs=[pl.BlockSpec((tm, d), lambda i: (i, 0)),
                      pl.BlockSpec((1, d), lambda i: (0, 0))],
            out_specs=pl.BlockSpec((tm, d), lambda i: (i, 0)),
            out_shape=jax.ShapeDtypeStruct((t, d), F32),
            compiler_params=_cparams(("parallel",)),
            name="final_norm",
        )(h, row(final_norm))
    return h.reshape(bsz, seq, d)
```

```python
import functools
import math

import jax
import jax.numpy as jnp
from jax import lax
from jax.experimental import pallas as pl
from jax.experimental.pallas import tpu as pltpu

F32 = jnp.float32
BF16 = jnp.bfloat16

GRID_W = 64
NA_HEADS = 8
NA_HEAD_DIM = 128
NA_WIN_ROWS = 8
NA_WIN_COLS = 16
NA_WIDTH = NA_HEADS * NA_HEAD_DIM
MLA_HEADS = 8
MLA_Q_RANK = 512
MLA_KV_RANK = 512
MLA_NOPE_DIM = 128
MLA_ROPE_DIM = 64
MLA_V_DIM = 128
MLA_QK_DIM = MLA_NOPE_DIM + MLA_ROPE_DIM
MLA_WIDTH = MLA_HEADS * MLA_V_DIM
ROPE_THETA = 10000.0
NORM_EPS = 1e-6
NEG_INF = -1e30
LOG2E = math.log2(math.e)

LANES = 128
BF16_ROWS = 16
MLA_HEAD_PAD = 2 * LANES
VMEM_LIMIT_BYTES = 56 * 1024 * 1024


def _cparams(sem):
    return pltpu.CompilerParams(dimension_semantics=sem, vmem_limit_bytes=VMEM_LIMIT_BYTES)


def _resident(shape):
    return pl.BlockSpec(shape, lambda *_: (0, 0), pipeline_mode=pl.Buffered(1))


def _rms(x, g):
    ms = jnp.mean(x * x, axis=-1, keepdims=True)
    return x * lax.rsqrt(ms + NORM_EPS) * g


def _sigmoid(x):
    return 1.0 / (1.0 + jnp.exp(-x))


def _chunk_rows(rows, nsteps):
    rc = BF16_ROWS
    while rows % rc or rows // rc > nsteps:
        rc += BF16_ROWS
    return rc


def _side_job_specs(casts, split_src, split, nsteps, lin):
    def spec(rows, cols):
        rc = _chunk_rows(rows, nsteps)
        last = rows // rc - 1
        return pl.BlockSpec((rc, cols), lambda *g: (jnp.minimum(lin(*g), last), 0))

    in_specs, out_specs, out_shape, operands = [], [], [], []
    for w in casts:
        in_specs.append(spec(*w.shape))
        out_specs.append(spec(*w.shape))
        out_shape.append(jax.ShapeDtypeStruct(w.shape, BF16))
        operands.append(w)
    if split_src is not None:
        rows, cols = split_src.shape
        in_specs.append(spec(rows, cols))
        out_specs += [spec(rows, cols - split[1]), spec(rows, LANES)]
        out_shape += [jax.ShapeDtypeStruct((rows, cols - split[1]), BF16),
                      jax.ShapeDtypeStruct((rows, LANES), BF16)]
        operands.append(split_src)
    return in_specs, out_specs, out_shape, operands


def _run_side_jobs(in_refs, out_refs, split):
    n_cast = len(in_refs) - (1 if split else 0)
    for src, dst in zip(in_refs[:n_cast], out_refs[:n_cast]):
        dst[...] = src[...].astype(BF16)
    if split:
        keep, skip = split
        main_ref, narrow_ref = out_refs[n_cast:]
        w = in_refs[n_cast][...]
        main_ref[:, :keep] = w[:, :keep].astype(BF16)
        main_ref[:, keep:] = w[:, keep + skip:].astype(BF16)
        pad = jnp.zeros((w.shape[0], LANES - skip), F32)
        narrow_ref[...] = jnp.concatenate([w[:, keep:keep + skip], pad], axis=1).astype(BF16)


def _ffn_body(*refs, n_side, split):
    x_ref, nw_ref, wg_ref, wu_ref, wd_ref = refs[:5]
    side_in = refs[5:5 + n_side]
    o_ref = refs[5 + n_side]
    side_out = refs[6 + n_side:-1]
    xn_ref = refs[-1]
    j = pl.program_id(1)

    @pl.when(j == 0)
    def _():
        x = x_ref[...]
        xn_ref[...] = _rms(x, nw_ref[...]).astype(BF16)
        o_ref[...] = x

    xn = xn_ref[...]
    g = jnp.dot(xn, wg_ref[...], preferred_element_type=F32)
    u = jnp.dot(xn, wu_ref[...], preferred_element_type=F32)
    a = (g * _sigmoid(g) * u * 0.5).astype(BF16)
    o_ref[...] += jnp.dot(a, wd_ref[...], preferred_element_type=F32)
    _run_side_jobs(side_in, side_out, split)


def _ffn(x, norm_w, wg, wu, wd, *, casts=(), split_src=None, split=None, tm=512, tf=512):
    t, d = x.shape
    f = wg.shape[1]
    nj = f // tf
    side = _side_job_specs(casts, split_src, split, (t // tm) * nj, lambda i, j: i * nj + j)
    outs = pl.pallas_call(
        functools.partial(_ffn_body, n_side=len(side[0]),
                          split=split if split_src is not None else None),
        grid=(t // tm, nj),
        in_specs=[
            pl.BlockSpec((tm, d), lambda i, j: (i, 0)),
            pl.BlockSpec((1, d), lambda i, j: (0, 0)),
            pl.BlockSpec((d, tf), lambda i, j: (0, j)),
            pl.BlockSpec((d, tf), lambda i, j: (0, j)),
            pl.BlockSpec((tf, d), lambda i, j: (j, 0)),
        ] + side[0],
        out_specs=[pl.BlockSpec((tm, d), lambda i, j: (i, 0))] + side[1],
        out_shape=[jax.ShapeDtypeStruct((t, d), F32)] + side[2],
        scratch_shapes=[pltpu.VMEM((tm, d), BF16)],
        compiler_params=_cparams(("arbitrary", "arbitrary")),
        name="ffn",
    )(x, norm_w, wg, wu, wd, *side[3])
    return outs[0], list(outs[1:])


def _in_proj_body(*refs, n_side):
    h_ref, nw_ref, w_ref, wkr_ref = refs[:4]
    side_in = refs[4:4 + n_side]
    z_ref, kr_ref = refs[4 + n_side:6 + n_side]
    side_out = refs[6 + n_side:-1]
    xn_ref = refs[-1]
    j = pl.program_id(1)

    @pl.when(j == 0)
    def _():
        xn = _rms(h_ref[...], nw_ref[...]).astype(BF16)
        xn_ref[...] = xn
        kr_ref[...] = jnp.dot(xn, wkr_ref[...], preferred_element_type=F32)

    z_ref[...] = jnp.dot(xn_ref[...], w_ref[...], preferred_element_type=F32).astype(BF16)
    _run_side_jobs(side_in, side_out, None)


def _in_proj(h, norm_w, w_main, w_kr, *, casts=(), tm=1024, tn=1024):
    t, d = h.shape
    n = w_main.shape[1]
    nj = n // tn
    side = _side_job_specs(casts, None, None, (t // tm) * nj, lambda i, j: i * nj + j)
    outs = pl.pallas_call(
        functools.partial(_in_proj_body, n_side=len(side[0])),
        grid=(t // tm, nj),
        in_specs=[
            pl.BlockSpec((tm, d), lambda i, j: (i, 0)),
            pl.BlockSpec((1, d), lambda i, j: (0, 0)),
            pl.BlockSpec((d, tn), lambda i, j: (0, j)),
            pl.BlockSpec((d, LANES), lambda i, j: (0, 0)),
        ] + side[0],
        out_specs=[
            pl.BlockSpec((tm, tn), lambda i, j: (i, j)),
            pl.BlockSpec((tm, LANES), lambda i, j: (i, 0)),
        ] + side[1],
        out_shape=[
            jax.ShapeDtypeStruct((t, n), BF16),
            jax.ShapeDtypeStruct((t, LANES), F32),
        ] + side[2],
        scratch_shapes=[pltpu.VMEM((tm, d), BF16)],
        compiler_params=_cparams(("arbitrary", "arbitrary")),
        name="in_proj",
    )(h, norm_w, w_main, w_kr, *side[3])
    return outs[0], outs[1], list(outs[2:])


def _rope(r, cos_t, sin_t):
    lane = lax.broadcasted_iota(jnp.int32, r.shape, 1)
    half = MLA_ROPE_DIM // 2
    swapped = jnp.where(lane < half, pltpu.roll(r, LANES - half, 1), pltpu.roll(r, half, 1))
    return r * cos_t + swapped * sin_t


def _mla_prep_body(ql_ref, kvl_ref, kr_ref, qn_ref, kvn_ref, wuq_ref, wukv_ref,
                   cos_ref, sin_ref, q_out, k_out, v_out):
    cq = _rms(ql_ref[...].astype(F32), qn_ref[...]).astype(BF16)
    ckv = _rms(kvl_ref[...].astype(F32), kvn_ref[...]).astype(BF16)
    cos_t = cos_ref[...]
    sin_t = sin_ref[...]
    k_rope = _rope(kr_ref[...], cos_t, sin_t).astype(BF16)
    for h in range(MLA_HEADS):
        lo = h * MLA_HEAD_PAD
        mid = lo + LANES
        hi = lo + MLA_HEAD_PAD
        q = jnp.dot(cq, wuq_ref[:, lo:hi], preferred_element_type=F32)
        kv = jnp.dot(ckv, wukv_ref[:, lo:hi], preferred_element_type=F32)
        q_out[:, lo:mid] = q[:, :LANES].astype(BF16)
        q_out[:, mid:hi] = _rope(q[:, LANES:], cos_t, sin_t).astype(BF16)
        k_out[:, lo:mid] = kv[:, :LANES].astype(BF16)
        k_out[:, mid:hi] = k_rope
        v_out[:, lo:mid] = kv[:, LANES:].astype(BF16)
        v_out[:, mid:hi] = jnp.ones((kv.shape[0], MLA_HEAD_PAD - MLA_V_DIM), BF16)


def _mla_prep(z, kr, q_norm, kv_norm, wuq_p, wukv, cos_t, sin_t, *, seq, ql_blk, kvl_blk, tm=512):
    t = z.shape[0]
    nseq = seq // tm
    qw = MLA_HEADS * MLA_HEAD_PAD
    return pl.pallas_call(
        _mla_prep_body,
        grid=(t // tm,),
        in_specs=[
            pl.BlockSpec((tm, MLA_Q_RANK), lambda i: (i, ql_blk)),
            pl.BlockSpec((tm, MLA_KV_RANK), lambda i: (i, kvl_blk)),
            pl.BlockSpec((tm, LANES), lambda i: (i, 0)),
            pl.BlockSpec((1, MLA_Q_RANK), lambda i: (0, 0)),
            pl.BlockSpec((1, MLA_KV_RANK), lambda i: (0, 0)),
            pl.BlockSpec((MLA_Q_RANK, qw), lambda i: (0, 0)),
            pl.BlockSpec((MLA_KV_RANK, qw), lambda i: (0, 0)),
            pl.BlockSpec((tm, LANES), lambda i: (i % nseq, 0)),
            pl.BlockSpec((tm, LANES), lambda i: (i % nseq, 0)),
        ],
        out_specs=[
            pl.BlockSpec((tm, qw), lambda i: (i, 0)),
            pl.BlockSpec((tm, qw), lambda i: (i, 0)),
            pl.BlockSpec((tm, qw), lambda i: (i, 0)),
        ],
        out_shape=[
            jax.ShapeDtypeStruct((t, qw), BF16),
            jax.ShapeDtypeStruct((t, qw), BF16),
            jax.ShapeDtypeStruct((t, qw), BF16),
        ],
        compiler_params=_cparams(("parallel",)),
        name="mla_prep",
    )(z, z, kr, q_norm, kv_norm, wuq_p, wukv, cos_t, sin_t)


RPB_ROWS = 2 * NA_WIN_ROWS - 1
RPB_COLS = 2 * NA_WIN_COLS - 1
NA_Q_ROWS = 4
NA_K_ROWS = NA_Q_ROWS + NA_WIN_ROWS
NA_VARIANTS = 3


def _na_variant_geometry(var, j):
    lo = (0, j, NA_K_ROWS - NA_WIN_ROWS)[var]
    rho0 = (NA_WIN_ROWS - 1) - j - (NA_WIN_ROWS // 2) * var
    return lo, rho0


def _na_bias_body(rpb_ref, o_ref, pair_ref):
    h = pl.program_id(0)
    kw = NA_WIN_COLS
    shape = (GRID_W, LANES)
    q = lax.broadcasted_iota(jnp.int32, shape, 0)
    lane = lax.broadcasted_iota(jnp.int32, shape, 1)
    first = lane < GRID_W
    k = jnp.where(first, lane, lane - GRID_W)
    col_start = jnp.clip(q - kw // 2, 0, GRID_W - kw)
    valid = (k >= col_start) & (k < col_start + kw)
    dc = jnp.clip(k - q, -(kw - 1), kw - 1) + (kw - 1)
    neg = jnp.full(shape, NEG_INF, F32)

    for e in range(RPB_ROWS + 1):
        base_a = (h * RPB_ROWS + max(e - 1, 0)) * RPB_COLS
        base_b = (h * RPB_ROWS + min(e, RPB_ROWS - 1)) * RPB_COLS
        acc = jnp.zeros(shape, F32)
        for c in range(RPB_COLS):
            val = jnp.where(first, rpb_ref[base_a + c], rpb_ref[base_b + c])
            acc = jnp.where(dc == c, val, acc)
        pair_ref[e] = jnp.where(valid, acc * LOG2E, NEG_INF)

    for var in range(NA_VARIANTS):
        for j in range(NA_Q_ROWS):
            lo, rho0 = _na_variant_geometry(var, j)
            for pair in range(NA_K_ROWS // 2):
                i = 2 * pair
                in_a = lo <= i < lo + NA_WIN_ROWS
                in_b = lo <= i + 1 < lo + NA_WIN_ROWS
                if in_a or in_b:
                    tile = pair_ref[rho0 + i + 1]
                    if not in_a:
                        tile = jnp.where(first, neg, tile)
                    if not in_b:
                        tile = jnp.where(first, tile, neg)
                else:
                    tile = neg
                o_ref[var, 0, j * GRID_W:(j + 1) * GRID_W, pair * LANES:(pair + 1) * LANES] = tile


def _na_bias(rpb_flat):
    nq = NA_Q_ROWS * GRID_W
    nk = NA_K_ROWS * GRID_W
    return pl.pallas_call(
        _na_bias_body,
        grid=(NA_HEADS,),
        in_specs=[pl.BlockSpec(memory_space=pltpu.SMEM)],
        out_specs=pl.BlockSpec((NA_VARIANTS, 1, nq, nk), lambda h: (0, h, 0, 0)),
        out_shape=jax.ShapeDtypeStruct((NA_VARIANTS, NA_HEADS, nq, nk), F32),
        scratch_shapes=[pltpu.VMEM((RPB_ROWS + 1, GRID_W, LANES), F32)],
        compiler_params=_cparams(("arbitrary",)),
        name="na_bias",
    )(rpb_flat)


def _na_key_start(g, rows):
    return jnp.clip(g * NA_Q_ROWS - NA_WIN_ROWS // 2, 0, rows - NA_K_ROWS)


def _na_body(q_ref, k_ref, v_ref, b_ref, o_ref, *, rows):
    g = pl.program_id(1)
    start = pl.multiple_of(_na_key_start(g, rows) * GRID_W, GRID_W)
    nkeys = NA_K_ROWS * GRID_W
    c = (NA_HEAD_DIM ** -0.5) * LOG2E
    ones = jnp.ones((nkeys, NA_HEAD_DIM), BF16)
    for h in range(NA_HEADS):
        cs = slice(h * NA_HEAD_DIM, (h + 1) * NA_HEAD_DIM)
        q = q_ref[0, :, cs]
        k = k_ref[0, pl.ds(start, nkeys), cs]
        v = v_ref[0, pl.ds(start, nkeys), cs]
        s = lax.dot_general(q, k, (((1,), (1,)), ((), ())), preferred_element_type=F32)
        t = s * c + b_ref[0, h]
        m = jnp.max(t, axis=-1, keepdims=True)
        p = jnp.exp2(t - m).astype(BF16)
        oe = jnp.dot(p, jnp.concatenate([v, ones], axis=1), preferred_element_type=F32)
        o_ref[0, :, cs] = (oe[:, :NA_HEAD_DIM] / oe[:, NA_HEAD_DIM:]).astype(BF16)


def _na_attn(z3, bias, *, q_blk, k_blk, v_blk):
    b, s, _ = z3.shape
    rows = s // GRID_W
    assert rows % NA_Q_ROWS == 0 and rows >= 2 * NA_K_ROWS - NA_WIN_ROWS
    groups = rows // NA_Q_ROWS
    nq = NA_Q_ROWS * GRID_W

    def bias_idx(bi, g):
        return (jnp.where(g == 0, 0, jnp.where(g == groups - 1, 2, 1)), 0, 0, 0)

    return pl.pallas_call(
        functools.partial(_na_body, rows=rows),
        grid=(b, groups),
        in_specs=[
            pl.BlockSpec((1, nq, NA_WIDTH), lambda bi, g: (bi, g, q_blk)),
            pl.BlockSpec((1, s, NA_WIDTH), lambda bi, g: (bi, 0, k_blk)),
            pl.BlockSpec((1, s, NA_WIDTH), lambda bi, g: (bi, 0, v_blk)),
            pl.BlockSpec((1, NA_HEADS, nq, NA_K_ROWS * GRID_W), bias_idx),
        ],
        out_specs=pl.BlockSpec((1, nq, NA_WIDTH), lambda bi, g: (bi, g, 0)),
        out_shape=jax.ShapeDtypeStruct((b, s, NA_WIDTH), BF16),
        compiler_params=_cparams(("parallel", "arbitrary")),
        name="na_attn",
    )(z3, z3, z3, bias)


def _mla_body(q_ref, k_ref, v_ref, o_ref, *, tq, tk):
    seq = q_ref.shape[1]
    c = (MLA_QK_DIM ** -0.5) * LOG2E
    for qi in range(seq // tq):
        q = q_ref[0, qi * tq:(qi + 1) * tq, :]
        m = acc = None
        for kj in range(seq // tk):
            ks = slice(kj * tk, (kj + 1) * tk)
            s = lax.dot_general(q, k_ref[0, ks, :], (((1,), (1,)), ((), ())),
                                preferred_element_type=F32) * c
            mj = jnp.max(s, axis=-1, keepdims=True)
            m_new = mj if m is None else jnp.maximum(m, mj)
            p = jnp.exp2(s - m_new).astype(BF16)
            pv = jnp.dot(p, v_ref[0, ks, :], preferred_element_type=F32)
            acc = pv if acc is None else acc * jnp.exp2(m - m_new) + pv
            m = m_new
        o_ref[0, qi * tq:(qi + 1) * tq, :] = (acc[:, :MLA_V_DIM] / acc[:, MLA_V_DIM:]).astype(BF16)


def _mla_attn(q3, k3, v3, *, tq=512, tk=512):
    b, s, _ = q3.shape
    vw = v3.shape[2] // MLA_HEADS
    return pl.pallas_call(
        functools.partial(_mla_body, tq=tq, tk=tk),
        grid=(b, MLA_HEADS),
        in_specs=[
            pl.BlockSpec((1, s, MLA_HEAD_PAD), lambda bi, h: (bi, 0, h)),
            pl.BlockSpec((1, s, MLA_HEAD_PAD), lambda bi, h: (bi, 0, h)),
            pl.BlockSpec((1, s, vw), lambda bi, h: (bi, 0, h)),
        ],
        out_specs=pl.BlockSpec((1, s, MLA_V_DIM), lambda bi, h: (bi, 0, h)),
        out_shape=jax.ShapeDtypeStruct((b, s, MLA_WIDTH), BF16),
        compiler_params=_cparams(("parallel", "arbitrary")),
        name="mla_attn",
    )(q3, k3, v3)


def _merge_body(na_ref, mla_ref, ga_ref, gb_ref, h_ref, wa_ref, wb_ref, wo_ref, o_ref):
    ya = jnp.dot(na_ref[...], wa_ref[...], preferred_element_type=F32)
    yb = jnp.dot(mla_ref[...], wb_ref[...], preferred_element_type=F32)
    merged = _sigmoid(ga_ref[...].astype(F32)) * ya + _sigmoid(gb_ref[...].astype(F32)) * yb
    o_ref[...] = h_ref[...] + jnp.dot(merged.astype(BF16), wo_ref[...],
                                      preferred_element_type=F32)


def _merge(na_o, mla_o, z, h, wa, wb, wo, *, ga_blk, gb_blk, tm=256):
    t, d = h.shape
    return pl.pallas_call(
        _merge_body,
        grid=(t // tm,),
        in_specs=[
            pl.BlockSpec((tm, NA_WIDTH), lambda i: (i, 0)),
            pl.BlockSpec((tm, MLA_WIDTH), lambda i: (i, 0)),
            pl.BlockSpec((tm, d), lambda i: (i, ga_blk)),
            pl.BlockSpec((tm, d), lambda i: (i, gb_blk)),
            pl.BlockSpec((tm, d), lambda i: (i, 0)),
            _resident((NA_WIDTH, d)),
            _resident((MLA_WIDTH, d)),
            _resident((d, d)),
        ],
        out_specs=pl.BlockSpec((tm, d), lambda i: (i, 0)),
        out_shape=jax.ShapeDtypeStruct((t, d), F32),
        compiler_params=_cparams(("parallel",)),
        name="merge",
    )(na_o, mla_o, z, z, h, wa, wb, wo)


def _pl_body(h_ref, p_ref, nw_ref, wg_ref, wp_ref, fn_ref, o_ref, *, final):
    h = h_ref[...]
    xn = _rms(h, nw_ref[...]).astype(BF16)
    gate = _sigmoid(jnp.dot(xn, wg_ref[...], preferred_element_type=F32))
    emb = jnp.dot(p_ref[...].astype(BF16), wp_ref[...], preferred_element_type=F32)
    h = h + gate * emb
    o_ref[...] = _rms(h, fn_ref[...]) if final else h


def _pl_embed(h, p, norm_w, w_gate, w_pl, final_norm, *, final, tm=512):
    t, d = h.shape
    pdim = p.shape[1]
    return pl.pallas_call(
        functools.partial(_pl_body, final=final),
        grid=(t // tm,),
        in_specs=[
            pl.BlockSpec((tm, d), lambda i: (i, 0)),
            pl.BlockSpec((tm, pdim), lambda i: (i, 0)),
            _resident((1, d)),
            _resident((d, d)),
            _resident((pdim, d)),
            _resident((1, d)),
        ],
        out_specs=pl.BlockSpec((tm, d), lambda i: (i, 0)),
        out_shape=jax.ShapeDtypeStruct((t, d), F32),
        compiler_params=_cparams(("parallel",)),
        name="pl_final" if final else "pl_embed",
    )(h, p, norm_w, w_gate, w_pl, final_norm)


def _final_norm_body(h_ref, fn_ref, o_ref):
    o_ref[...] = _rms(h_ref[...], fn_ref[...])


def _rope_tables(seq):
    half = MLA_ROPE_DIM // 2
    pos = jnp.arange(seq, dtype=F32)
    inv_freq = 1.0 / (ROPE_THETA ** (jnp.arange(0, MLA_ROPE_DIM, 2, dtype=F32) / MLA_ROPE_DIM))
    ang = pos[:, None] * inv_freq[None, :]
    cos, sin = jnp.cos(ang), jnp.sin(ang)
    zeros = jnp.zeros((seq, LANES - 2 * half), F32)
    return (jnp.concatenate([cos, cos, zeros], axis=-1),
            jnp.concatenate([-sin, sin, zeros], axis=-1))


def kernel(x, p, ffn1_norm, ffn1_w_gate, ffn1_w_up, ffn1_w_down, mix_norm, w_in, q_a_norm, w_uq, kv_a_norm, w_ukv, na_rpb, w_branch_a, w_branch_b, w_out, ffn2_norm, ffn2_w_gate, ffn2_w_up, ffn2_w_down, pl_norm, w_pl, w_pl_gate, final_norm):
    bsz, seq, d = x.shape
    depth = p.shape[0]
    t = bsz * seq
    bf = lambda w: w.astype(BF16)
    row = lambda g: g.reshape(1, -1).astype(F32)

    c_lat = 3 * NA_WIDTH
    c_rope = c_lat + MLA_Q_RANK + MLA_KV_RANK
    cos_t, sin_t = _rope_tables(seq)

    h = x.reshape(t, d)
    for i in range(depth):
        h, (w_main, w_kr) = _ffn(h, row(ffn1_norm[i]), bf(ffn1_w_gate[i]), bf(ffn1_w_up[i]),
                                 bf(ffn1_w_down[i]), split_src=w_in[i], split=(c_rope, MLA_ROPE_DIM))
        later = [ffn2_w_gate[i], ffn2_w_up[i], ffn2_w_down[i], w_branch_a[i], w_branch_b[i],
                 w_out[i], w_pl_gate[i]]
        z, kr, (w2_gate, w2_up, w2_down, wa, wb, wo, w_plg) = _in_proj(
            h, row(mix_norm[i]), w_main, w_kr, casts=later)

        wuq = w_uq[i].reshape(MLA_Q_RANK, MLA_HEADS, MLA_QK_DIM)
        wuq_p = bf(jnp.pad(wuq, ((0, 0), (0, 0), (0, MLA_HEAD_PAD - MLA_QK_DIM)))
                   .reshape(MLA_Q_RANK, MLA_HEADS * MLA_HEAD_PAD))
        q_m, k_m, v_m = _mla_prep(
            z, kr, row(q_a_norm[i]), row(kv_a_norm[i]), wuq_p, bf(w_ukv[i]), cos_t, sin_t,
            seq=seq, ql_blk=c_lat // MLA_Q_RANK, kvl_blk=(c_lat + MLA_Q_RANK) // MLA_KV_RANK)

        bias = _na_bias(na_rpb[i].reshape(-1).astype(F32))
        z3 = z.reshape(bsz, seq, z.shape[1])
        na_o = _na_attn(z3, bias, q_blk=0, k_blk=1, v_blk=2)
        mla_o = _mla_attn(q_m.reshape(bsz, seq, -1), k_m.reshape(bsz, seq, -1),
                          v_m.reshape(bsz, seq, -1))

        gate_col = c_rope
        h = _merge(na_o.reshape(t, NA_WIDTH), mla_o.reshape(t, MLA_WIDTH), z, h, wa, wb, wo,
                   ga_blk=gate_col // d, gb_blk=gate_col // d + 1)

        h, _ = _ffn(h, row(ffn2_norm[i]), w2_gate, w2_up, w2_down)

        h = _pl_embed(h, p[i].reshape(t, -1), row(pl_norm[i]), w_plg, bf(w_pl[i]),
                      row(final_norm), final=(i == depth - 1))

    if depth == 0:
        tm = 512
        h = pl.pallas_call(
            _final_norm_body,
            grid=(t // tm,),
            in_specs=[pl.BlockSpec((tm, d), lambda i: (i, 0)),
                      pl.BlockSpec((1, d), lambda i: (0, 0))],
            out_specs=pl.BlockSpec((tm, d), lambda i: (i, 0)),
            out_shape=jax.ShapeDtypeStruct((t, d), F32),
            compiler_params=_cparams(("parallel",)),
            name="final_norm",
        )(h, row(final_norm))
    return h.reshape(bsz, seq, d)
```

```python
import functools
import math

import jax
import jax.numpy as jnp
from jax import lax
from jax.experimental import pallas as pl
from jax.experimental.pallas import tpu as pltpu

F32 = jnp.float32
BF16 = jnp.bfloat16

GRID_W = 64
NA_HEADS = 8
NA_HEAD_DIM = 128
NA_WIN_ROWS = 8
NA_WIN_COLS = 16
NA_WIDTH = NA_HEADS * NA_HEAD_DIM
MLA_HEADS = 8
MLA_Q_RANK = 512
MLA_KV_RANK = 512
MLA_NOPE_DIM = 128
MLA_ROPE_DIM = 64
MLA_V_DIM = 128
MLA_QK_DIM = MLA_NOPE_DIM + MLA_ROPE_DIM
MLA_WIDTH = MLA_HEADS * MLA_V_DIM
ROPE_THETA = 10000.0
NORM_EPS = 1e-6
NEG_INF = -1e30
LOG2E = math.log2(math.e)

LANES = 128
BF16_ROWS = 16
MLA_HEAD_PAD = 2 * LANES
VMEM_LIMIT_BYTES = 56 * 1024 * 1024


def _cparams(sem):
    return pltpu.CompilerParams(dimension_semantics=sem, vmem_limit_bytes=VMEM_LIMIT_BYTES)


def _resident(shape):
    return pl.BlockSpec(shape, lambda *_: (0, 0), pipeline_mode=pl.Buffered(1))


def _rms(x, g):
    ms = jnp.mean(x * x, axis=-1, keepdims=True)
    return x * lax.rsqrt(ms + NORM_EPS) * g


def _sigmoid(x):
    return 1.0 / (1.0 + jnp.exp(-x))


def _dot_nt(a, b):
    return lax.dot_general(a, b, (((1,), (1,)), ((), ())), preferred_element_type=F32)


def _chunk_rows(rows, nsteps):
    rc = BF16_ROWS
    while rows % rc or rows // rc > nsteps:
        rc += BF16_ROWS
    return rc


def _side_job_specs(casts, split_src, split, nsteps, lin):
    def spec(rows, cols):
        rc = _chunk_rows(rows, nsteps)
        last = rows // rc - 1
        return pl.BlockSpec((rc, cols), lambda *g: (jnp.minimum(lin(*g), last), 0))

    in_specs, out_specs, out_shape, operands = [], [], [], []
    for w in casts:
        in_specs.append(spec(*w.shape))
        out_specs.append(spec(*w.shape))
        out_shape.append(jax.ShapeDtypeStruct(w.shape, BF16))
        operands.append(w)
    if split_src is not None:
        rows, cols = split_src.shape
        keep, skip = split
        assert keep % skip == 0 and rows % skip == 0 and rows // skip <= nsteps and skip <= LANES
        last, hole = rows // skip - 1, keep // skip

        def packed_idx(*g):
            c = jnp.minimum(lin(*g), last)
            return (jnp.where(c <= hole, jnp.minimum(c, hole - 1), c - 1), 0)

        in_specs.append(pl.BlockSpec((skip, cols), lambda *g: (jnp.minimum(lin(*g), last), 0)))
        out_specs += [pl.BlockSpec((skip, cols), packed_idx),
                      pl.BlockSpec((LANES, cols), lambda *g: (0, 0))]
        out_shape += [jax.ShapeDtypeStruct((rows - skip, cols), BF16),
                      jax.ShapeDtypeStruct((LANES, cols), BF16)]
        operands.append(split_src)
    return in_specs, out_specs, out_shape, operands


def _run_side_jobs(in_refs, out_refs, split, step):
    n_cast = len(in_refs) - (1 if split else 0)
    for src, dst in zip(in_refs[:n_cast], out_refs[:n_cast]):
        dst[...] = src[...].astype(BF16)
    if split:
        keep, skip = split
        src = in_refs[n_cast]
        packed_ref, moved_ref = out_refs[n_cast:]
        hole = keep // skip

        @pl.when(step != hole)
        def _():
            packed_ref[...] = src[...].astype(BF16)

        @pl.when(step == hole)
        def _():
            moved_ref[:skip, :] = src[...].astype(BF16)
            if skip < LANES:
                moved_ref[skip:, :] = jnp.zeros((LANES - skip, src.shape[1]), BF16)


def _ffn_up_body(*refs, n_side, split):
    x_ref, nw_ref, wg_ref, wu_ref = refs[:4]
    side_in = refs[4:4 + n_side]
    a_ref = refs[4 + n_side]
    side_out = refs[5 + n_side:-1]
    xn_ref = refs[-1]
    j = pl.program_id(1)

    @pl.when(j == 0)
    def _():
        xn_ref[...] = _rms(x_ref[...], nw_ref[...]).astype(BF16)

    xn = xn_ref[...]
    g = jnp.dot(xn, wg_ref[...], preferred_element_type=F32)
    u = jnp.dot(xn, wu_ref[...], preferred_element_type=F32)
    a_ref[...] = (g * _sigmoid(g) * u * 0.5).astype(BF16)
    _run_side_jobs(side_in, side_out, split, pl.program_id(0) * pl.num_programs(1) + j)


def _ffn_down_body(a_ref, wd_ref, x_ref, o_ref):
    o_ref[...] = x_ref[...] + jnp.dot(a_ref[...], wd_ref[...], preferred_element_type=F32)


def _ffn(x, norm_w, wg, wu, wd, *, casts=(), split_src=None, split=None, tm=1024, tf=512, tn=512):
    t, d = x.shape
    f = wg.shape[1]
    nj = f // tf
    side = _side_job_specs(casts, split_src, split, (t // tm) * nj, lambda i, j: i * nj + j)
    outs = pl.pallas_call(
        functools.partial(_ffn_up_body, n_side=len(side[0]),
                          split=split if split_src is not None else None),
        grid=(t // tm, nj),
        in_specs=[
            pl.BlockSpec((tm, d), lambda i, j: (i, 0)),
            pl.BlockSpec((1, d), lambda i, j: (0, 0)),
            pl.BlockSpec((d, tf), lambda i, j: (0, j)),
            pl.BlockSpec((d, tf), lambda i, j: (0, j)),
        ] + side[0],
        out_specs=[pl.BlockSpec((tm, tf), lambda i, j: (i, j))] + side[1],
        out_shape=[jax.ShapeDtypeStruct((t, f), BF16)] + side[2],
        scratch_shapes=[pltpu.VMEM((tm, d), BF16)],
        compiler_params=_cparams(("arbitrary", "arbitrary")),
        name="ffn_up",
    )(x, norm_w, wg, wu, *side[3])
    h = pl.pallas_call(
        _ffn_down_body,
        grid=(t // tm, d // tn),
        in_specs=[
            pl.BlockSpec((tm, f), lambda i, j: (i, 0)),
            pl.BlockSpec((f, tn), lambda i, j: (0, j)),
            pl.BlockSpec((tm, tn), lambda i, j: (i, j)),
        ],
        out_specs=pl.BlockSpec((tm, tn), lambda i, j: (i, j)),
        out_shape=jax.ShapeDtypeStruct((t, d), F32),
        compiler_params=_cparams(("parallel", "arbitrary")),
        name="ffn_down",
    )(outs[0], wd, x)
    return h, list(outs[1:])


def _in_proj_body(*refs, n_side):
    h_ref, nw_ref, w_ref, wkr_ref = refs[:4]
    side_in = refs[4:4 + n_side]
    z_ref, kr_ref = refs[4 + n_side:6 + n_side]
    side_out = refs[6 + n_side:-1]
    xn_ref = refs[-1]
    j = pl.program_id(1)

    @pl.when(j == 0)
    def _():
        xn = _rms(h_ref[...], nw_ref[...]).astype(BF16)
        xn_ref[...] = xn
        kr_ref[...] = _dot_nt(xn, wkr_ref[...])

    z_ref[...] = _dot_nt(xn_ref[...], w_ref[...]).astype(BF16)
    _run_side_jobs(side_in, side_out, None, None)


def _in_proj(h, norm_w, w_main_t, w_kr_t, *, casts=(), tm=1024, tn=1024):
    t, d = h.shape
    n = w_main_t.shape[0]
    nj = n // tn
    side = _side_job_specs(casts, None, None, (t // tm) * nj, lambda i, j: i * nj + j)
    outs = pl.pallas_call(
        functools.partial(_in_proj_body, n_side=len(side[0])),
        grid=(t // tm, nj),
        in_specs=[
            pl.BlockSpec((tm, d), lambda i, j: (i, 0)),
            pl.BlockSpec((1, d), lambda i, j: (0, 0)),
            pl.BlockSpec((tn, d), lambda i, j: (j, 0)),
            pl.BlockSpec((LANES, d), lambda i, j: (0, 0)),
        ] + side[0],
        out_specs=[
            pl.BlockSpec((tm, tn), lambda i, j: (i, j)),
            pl.BlockSpec((tm, LANES), lambda i, j: (i, 0)),
        ] + side[1],
        out_shape=[
            jax.ShapeDtypeStruct((t, n), BF16),
            jax.ShapeDtypeStruct((t, LANES), F32),
        ] + side[2],
        scratch_shapes=[pltpu.VMEM((tm, d), BF16)],
        compiler_params=_cparams(("arbitrary", "arbitrary")),
        name="in_proj",
    )(h, norm_w, w_main_t, w_kr_t, *side[3])
    return outs[0], outs[1], list(outs[2:])


def _rope(r, cos_t, sin_t):
    lane = lax.broadcasted_iota(jnp.int32, r.shape, 1)
    half = MLA_ROPE_DIM // 2
    swapped = jnp.where(lane < half, pltpu.roll(r, LANES - half, 1), pltpu.roll(r, half, 1))
    return r * cos_t + swapped * sin_t


def _mla_prep_body(ql_ref, kvl_ref, kr_ref, qn_ref, kvn_ref, wuq_ref, wukv_ref,
                   cos_ref, sin_ref, q_out, k_out, v_out):
    cq = _rms(ql_ref[...].astype(F32), qn_ref[...]).astype(BF16)
    ckv = _rms(kvl_ref[...].astype(F32), kvn_ref[...]).astype(BF16)
    cos_t = cos_ref[...]
    sin_t = sin_ref[...]
    k_rope = _rope(kr_ref[...], cos_t, sin_t).astype(BF16)
    for h in range(MLA_HEADS):
        lo = h * MLA_HEAD_PAD
        mid = lo + LANES
        hi = lo + MLA_HEAD_PAD
        q = jnp.dot(cq, wuq_ref[:, lo:hi], preferred_element_type=F32)
        kv = jnp.dot(ckv, wukv_ref[:, lo:hi], preferred_element_type=F32)
        q_out[:, lo:mid] = q[:, :LANES].astype(BF16)
        q_out[:, mid:hi] = _rope(q[:, LANES:], cos_t, sin_t).astype(BF16)
        k_out[:, lo:mid] = kv[:, :LANES].astype(BF16)
        k_out[:, mid:hi] = k_rope
        v_out[:, lo:mid] = kv[:, LANES:].astype(BF16)
        v_out[:, mid:hi] = jnp.ones((kv.shape[0], MLA_HEAD_PAD - MLA_V_DIM), BF16)


def _mla_prep(z, kr, q_norm, kv_norm, wuq_p, wukv, cos_t, sin_t, *, seq, ql_blk, kvl_blk, tm=512):
    t = z.shape[0]
    nseq = seq // tm
    qw = MLA_HEADS * MLA_HEAD_PAD
    return pl.pallas_call(
        _mla_prep_body,
        grid=(t // tm,),
        in_specs=[
            pl.BlockSpec((tm, MLA_Q_RANK), lambda i: (i, ql_blk)),
            pl.BlockSpec((tm, MLA_KV_RANK), lambda i: (i, kvl_blk)),
            pl.BlockSpec((tm, LANES), lambda i: (i, 0)),
            pl.BlockSpec((1, MLA_Q_RANK), lambda i: (0, 0)),
            pl.BlockSpec((1, MLA_KV_RANK), lambda i: (0, 0)),
            pl.BlockSpec((MLA_Q_RANK, qw), lambda i: (0, 0)),
            pl.BlockSpec((MLA_KV_RANK, qw), lambda i: (0, 0)),
            pl.BlockSpec((tm, LANES), lambda i: (i % nseq, 0)),
            pl.BlockSpec((tm, LANES), lambda i: (i % nseq, 0)),
        ],
        out_specs=[
            pl.BlockSpec((tm, qw), lambda i: (i, 0)),
            pl.BlockSpec((tm, qw), lambda i: (i, 0)),
            pl.BlockSpec((tm, qw), lambda i: (i, 0)),
        ],
        out_shape=[
            jax.ShapeDtypeStruct((t, qw), BF16),
            jax.ShapeDtypeStruct((t, qw), BF16),
            jax.ShapeDtypeStruct((t, qw), BF16),
        ],
        compiler_params=_cparams(("parallel",)),
        name="mla_prep",
    )(z, z, kr, q_norm, kv_norm, wuq_p, wukv, cos_t, sin_t)


RPB_ROWS = 2 * NA_WIN_ROWS - 1
RPB_COLS = 2 * NA_WIN_COLS - 1
NA_Q_ROWS = 4
NA_K_ROWS = NA_Q_ROWS + NA_WIN_ROWS
NA_VARIANTS = 3


def _na_variant_geometry(var, j):
    lo = (0, j, NA_K_ROWS - NA_WIN_ROWS)[var]
    rho0 = (NA_WIN_ROWS - 1) - j - (NA_WIN_ROWS // 2) * var
    return lo, rho0


def _na_bias_body(rpb_ref, o_ref, pair_ref):
    h = pl.program_id(0)
    kw = NA_WIN_COLS
    shape = (GRID_W, LANES)
    q = lax.broadcasted_iota(jnp.int32, shape, 0)
    lane = lax.broadcasted_iota(jnp.int32, shape, 1)
    first = lane < GRID_W
    k = jnp.where(first, lane, lane - GRID_W)
    col_start = jnp.clip(q - kw // 2, 0, GRID_W - kw)
    valid = (k >= col_start) & (k < col_start + kw)
    dc = jnp.clip(k - q, -(kw - 1), kw - 1) + (kw - 1)
    neg = jnp.full(shape, NEG_INF, F32)

    for e in range(RPB_ROWS + 1):
        base_a = (h * RPB_ROWS + max(e - 1, 0)) * RPB_COLS
        base_b = (h * RPB_ROWS + min(e, RPB_ROWS - 1)) * RPB_COLS
        acc = jnp.zeros(shape, F32)
        for c in range(RPB_COLS):
            val = jnp.where(first, rpb_ref[base_a + c], rpb_ref[base_b + c])
            acc = jnp.where(dc == c, val, acc)
        pair_ref[e] = jnp.where(valid, acc * LOG2E, NEG_INF)

    for var in range(NA_VARIANTS):
        for j in range(NA_Q_ROWS):
            lo, rho0 = _na_variant_geometry(var, j)
            for pair in range(NA_K_ROWS // 2):
                i = 2 * pair
                in_a = lo <= i < lo + NA_WIN_ROWS
                in_b = lo <= i + 1 < lo + NA_WIN_ROWS
                if in_a or in_b:
                    tile = pair_ref[rho0 + i + 1]
                    if not in_a:
                        tile = jnp.where(first, neg, tile)
                    if not in_b:
                        tile = jnp.where(first, tile, neg)
                else:
                    tile = neg
                o_ref[var, 0, j * GRID_W:(j + 1) * GRID_W, pair * LANES:(pair + 1) * LANES] = tile


def _na_bias(rpb_flat):
    nq = NA_Q_ROWS * GRID_W
    nk = NA_K_ROWS * GRID_W
    return pl.pallas_call(
        _na_bias_body,
        grid=(NA_HEADS,),
        in_specs=[pl.BlockSpec(memory_space=pltpu.SMEM)],
        out_specs=pl.BlockSpec((NA_VARIANTS, 1, nq, nk), lambda h: (0, h, 0, 0)),
        out_shape=jax.ShapeDtypeStruct((NA_VARIANTS, NA_HEADS, nq, nk), F32),
        scratch_shapes=[pltpu.VMEM((RPB_ROWS + 1, GRID_W, LANES), F32)],
        compiler_params=_cparams(("arbitrary",)),
        name="na_bias",
    )(rpb_flat)


def _na_key_start(g, rows):
    return jnp.clip(g * NA_Q_ROWS - NA_WIN_ROWS // 2, 0, rows - NA_K_ROWS)


def _na_body(q_ref, k_ref, v_ref, b_ref, o_ref, *, rows):
    g = pl.program_id(1)
    start = pl.multiple_of(_na_key_start(g, rows) * GRID_W, GRID_W)
    kind = jnp.where(g == 0, 0, jnp.where(g == rows // NA_Q_ROWS - 1, 2, 1))
    nkeys = NA_K_ROWS * GRID_W
    c = (NA_HEAD_DIM ** -0.5) * LOG2E
    ones = jnp.ones((nkeys, NA_HEAD_DIM), BF16)
    for h in range(NA_HEADS):
        cs = slice(h * NA_HEAD_DIM, (h + 1) * NA_HEAD_DIM)
        q = q_ref[0, :, cs]
        k = k_ref[0, pl.ds(start, nkeys), cs]
        v = v_ref[0, pl.ds(start, nkeys), cs]
        s = _dot_nt(q, k)
        t = s * c + b_ref[kind, h]
        m = jnp.max(t, axis=-1, keepdims=True)
        p = jnp.exp2(t - m).astype(BF16)
        oe = jnp.dot(p, jnp.concatenate([v, ones], axis=1), preferred_element_type=F32)
        o_ref[0, :, cs] = (oe[:, :NA_HEAD_DIM] / oe[:, NA_HEAD_DIM:]).astype(BF16)


def _na_attn(z3, bias, *, q_blk, k_blk, v_blk):
    b, s, _ = z3.shape
    rows = s // GRID_W
    assert rows % NA_Q_ROWS == 0 and rows >= 2 * NA_K_ROWS - NA_WIN_ROWS
    groups = rows // NA_Q_ROWS
    nq = NA_Q_ROWS * GRID_W
    return pl.pallas_call(
        functools.partial(_na_body, rows=rows),
        grid=(b, groups),
        in_specs=[
            pl.BlockSpec((1, nq, NA_WIDTH), lambda bi, g: (bi, g, q_blk)),
            pl.BlockSpec((1, s, NA_WIDTH), lambda bi, g: (bi, 0, k_blk)),
            pl.BlockSpec((1, s, NA_WIDTH), lambda bi, g: (bi, 0, v_blk)),
            pl.BlockSpec(bias.shape, lambda bi, g: (0, 0, 0, 0), pipeline_mode=pl.Buffered(1)),
        ],
        out_specs=pl.BlockSpec((1, nq, NA_WIDTH), lambda bi, g: (bi, g, 0)),
        out_shape=jax.ShapeDtypeStruct((b, s, NA_WIDTH), BF16),
        compiler_params=_cparams(("parallel", "arbitrary")),
        name="na_attn",
    )(z3, z3, z3, bias)


def _mla_body(q_ref, k_ref, v_ref, o_ref, *, tq, tk):
    seq = q_ref.shape[1]
    c = (MLA_QK_DIM ** -0.5) * LOG2E
    for qi in range(seq // tq):
        q = q_ref[0, qi * tq:(qi + 1) * tq, :]
        m = acc = None
        for kj in range(seq // tk):
            ks = slice(kj * tk, (kj + 1) * tk)
            s = _dot_nt(q, k_ref[0, ks, :]) * c
            mj = jnp.max(s, axis=-1, keepdims=True)
            m_new = mj if m is None else jnp.maximum(m, mj)
            p = jnp.exp2(s - m_new).astype(BF16)
            pv = jnp.dot(p, v_ref[0, ks, :], preferred_element_type=F32)
            acc = pv if acc is None else acc * jnp.exp2(m - m_new) + pv
            m = m_new
        o_ref[0, qi * tq:(qi + 1) * tq, :] = (acc[:, :MLA_V_DIM] / acc[:, MLA_V_DIM:]).astype(BF16)


def _mla_attn(q3, k3, v3, *, tq=512, tk=512):
    b, s, _ = q3.shape
    vw = v3.shape[2] // MLA_HEADS
    return pl.pallas_call(
        functools.partial(_mla_body, tq=tq, tk=tk),
        grid=(b, MLA_HEADS),
        in_specs=[
            pl.BlockSpec((1, s, MLA_HEAD_PAD), lambda bi, h: (bi, 0, h)),
            pl.BlockSpec((1, s, MLA_HEAD_PAD), lambda bi, h: (bi, 0, h)),
            pl.BlockSpec((1, s, vw), lambda bi, h: (bi, 0, h)),
        ],
        out_specs=pl.BlockSpec((1, s, MLA_V_DIM), lambda bi, h: (bi, 0, h)),
        out_shape=jax.ShapeDtypeStruct((b, s, MLA_WIDTH), BF16),
        compiler_params=_cparams(("parallel", "arbitrary")),
        name="mla_attn",
    )(q3, k3, v3)


def _merge_body(na_ref, mla_ref, ga_ref, gb_ref, h_ref, wa_ref, wb_ref, wo_ref, o_ref):
    ya = jnp.dot(na_ref[...], wa_ref[...], preferred_element_type=F32)
    yb = jnp.dot(mla_ref[...], wb_ref[...], preferred_element_type=F32)
    merged = _sigmoid(ga_ref[...].astype(F32)) * ya + _sigmoid(gb_ref[...].astype(F32)) * yb
    o_ref[...] = h_ref[...] + jnp.dot(merged.astype(BF16), wo_ref[...],
                                      preferred_element_type=F32)


def _merge(na_o, mla_o, z, h, wa, wb, wo, *, ga_blk, gb_blk, tm=256):
    t, d = h.shape
    return pl.pallas_call(
        _merge_body,
        grid=(t // tm,),
        in_specs=[
            pl.BlockSpec((tm, NA_WIDTH), lambda i: (i, 0)),
            pl.BlockSpec((tm, MLA_WIDTH), lambda i: (i, 0)),
            pl.BlockSpec((tm, d), lambda i: (i, ga_blk)),
            pl.BlockSpec((tm, d), lambda i: (i, gb_blk)),
            pl.BlockSpec((tm, d), lambda i: (i, 0)),
            _resident((NA_WIDTH, d)),
            _resident((MLA_WIDTH, d)),
            _resident((d, d)),
        ],
        out_specs=pl.BlockSpec((tm, d), lambda i: (i, 0)),
        out_shape=jax.ShapeDtypeStruct((t, d), F32),
        compiler_params=_cparams(("parallel",)),
        name="merge",
    )(na_o, mla_o, z, z, h, wa, wb, wo)


def _pl_body(h_ref, p_ref, nw_ref, wg_ref, wp_ref, fn_ref, o_ref, *, final):
    h = h_ref[...]
    xn = _rms(h, nw_ref[...]).astype(BF16)
    gate = _sigmoid(jnp.dot(xn, wg_ref[...], preferred_element_type=F32))
    emb = jnp.dot(p_ref[...].astype(BF16), wp_ref[...], preferred_element_type=F32)
    h = h + gate * emb
    o_ref[...] = _rms(h, fn_ref[...]) if final else h


def _pl_embed(h, p, norm_w, w_gate, w_pl, final_norm, *, final, tm=512):
    t, d = h.shape
    pdim = p.shape[1]
    return pl.pallas_call(
        functools.partial(_pl_body, final=final),
        grid=(t // tm,),
        in_specs=[
            pl.BlockSpec((tm, d), lambda i: (i, 0)),
            pl.BlockSpec((tm, pdim), lambda i: (i, 0)),
            _resident((1, d)),
            _resident((d, d)),
            _resident((pdim, d)),
            _resident((1, d)),
        ],
        out_specs=pl.BlockSpec((tm, d), lambda i: (i, 0)),
        out_shape=jax.ShapeDtypeStruct((t, d), F32),
        compiler_params=_cparams(("parallel",)),
        name="pl_final" if final else "pl_embed",
    )(h, p, norm_w, w_gate, w_pl, final_norm)


def _final_norm_body(h_ref, fn_ref, o_ref):
    o_ref[...] = _rms(h_ref[...], fn_ref[...])


def _rope_tables(seq):
    half = MLA_ROPE_DIM // 2
    pos = jnp.arange(seq, dtype=F32)
    inv_freq = 1.0 / (ROPE_THETA ** (jnp.arange(0, MLA_ROPE_DIM, 2, dtype=F32) / MLA_ROPE_DIM))
    ang = pos[:, None] * inv_freq[None, :]
    cos, sin = jnp.cos(ang), jnp.sin(ang)
    zeros = jnp.zeros((seq, LANES - 2 * half), F32)
    return (jnp.concatenate([cos, cos, zeros], axis=-1),
            jnp.concatenate([-sin, sin, zeros], axis=-1))


def kernel(x, p, ffn1_norm, ffn1_w_gate, ffn1_w_up, ffn1_w_down, mix_norm, w_in, q_a_norm, w_uq, kv_a_norm, w_ukv, na_rpb, w_branch_a, w_branch_b, w_out, ffn2_norm, ffn2_w_gate, ffn2_w_up, ffn2_w_down, pl_norm, w_pl, w_pl_gate, final_norm):
    bsz, seq, d = x.shape
    depth = p.shape[0]
    t = bsz * seq
    bf = lambda w: w.astype(BF16)
    row = lambda g: g.reshape(1, -1).astype(F32)

    c_lat = 3 * NA_WIDTH
    c_rope = c_lat + MLA_Q_RANK + MLA_KV_RANK
    cos_t, sin_t = _rope_tables(seq)

    h = x.reshape(t, d)
    for i in range(depth):
        h, (w_main_t, w_kr_t) = _ffn(
            h, row(ffn1_norm[i]), bf(ffn1_w_gate[i]), bf(ffn1_w_up[i]), bf(ffn1_w_down[i]),
            split_src=jnp.swapaxes(w_in[i], 0, 1), split=(c_rope, MLA_ROPE_DIM))
        later = [ffn2_w_gate[i], ffn2_w_up[i], ffn2_w_down[i], w_branch_a[i], w_branch_b[i],
                 w_out[i], w_pl_gate[i]]
        z, kr, (w2_gate, w2_up, w2_down, wa, wb, wo, w_plg) = _in_proj(
            h, row(mix_norm[i]), w_main_t, w_kr_t, casts=later)

        wuq = w_uq[i].reshape(MLA_Q_RANK, MLA_HEADS, MLA_QK_DIM)
        wuq_p = bf(jnp.pad(wuq, ((0, 0), (0, 0), (0, MLA_HEAD_PAD - MLA_QK_DIM)))
                   .reshape(MLA_Q_RANK, MLA_HEADS * MLA_HEAD_PAD))
        q_m, k_m, v_m = _mla_prep(
            z, kr, row(q_a_norm[i]), row(kv_a_norm[i]), wuq_p, bf(w_ukv[i]), cos_t, sin_t,
            seq=seq, ql_blk=c_lat // MLA_Q_RANK, kvl_blk=(c_lat + MLA_Q_RANK) // MLA_KV_RANK)

        bias = _na_bias(na_rpb[i].reshape(-1).astype(F32))
        z3 = z.reshape(bsz, seq, z.shape[1])
        na_o = _na_attn(z3, bias, q_blk=0, k_blk=1, v_blk=2)
        mla_o = _mla_attn(q_m.reshape(bsz, seq, -1), k_m.reshape(bsz, seq, -1),
                          v_m.reshape(bsz, seq, -1))

        gate_col = c_rope
        h = _merge(na_o.reshape(t, NA_WIDTH), mla_o.reshape(t, MLA_WIDTH), z, h, wa, wb, wo,
                   ga_blk=gate_col // d, gb_blk=gate_col // d + 1)

        h, _ = _ffn(h, row(ffn2_norm[i]), w2_gate, w2_up, w2_down)

        h = _pl_embed(h, p[i].reshape(t, -1), row(pl_norm[i]), w_plg, bf(w_pl[i]),
                      row(final_norm), final=(i == depth - 1))

    if depth == 0:
        tm = 512
        h = pl.pallas_call(
            _final_norm_body,
            grid=(t // tm,),
            in_specs=[pl.BlockSpec((tm, d), lambda i: (i, 0)),
                      pl.BlockSpec((1, d), lambda i: (0, 0))],
            out_specs=pl.BlockSpec((tm, d), lambda i: (i, 0)),
            out_shape=jax.ShapeDtypeStruct((t, d), F32),
            compiler_params=_cparams(("parallel",)),
            name="final_norm",
        )(h, row(final_norm))
    return h.reshape(bsz, seq, d)
```

```python
import functools
import math

import jax
import jax.numpy as jnp
from jax import lax
from jax.experimental import pallas as pl
from jax.experimental.pallas import tpu as pltpu

F32 = jnp.float32
BF16 = jnp.bfloat16

GRID_W = 64
NA_HEADS = 8
NA_HEAD_DIM = 128
NA_WIN_ROWS = 8
NA_WIN_COLS = 16
NA_WIDTH = NA_HEADS * NA_HEAD_DIM
MLA_HEADS = 8
MLA_Q_RANK = 512
MLA_KV_RANK = 512
MLA_NOPE_DIM = 128
MLA_ROPE_DIM = 64
MLA_V_DIM = 128
MLA_QK_DIM = MLA_NOPE_DIM + MLA_ROPE_DIM
MLA_WIDTH = MLA_HEADS * MLA_V_DIM
ROPE_THETA = 10000.0
NORM_EPS = 1e-6
NEG_INF = -1e30
LOG2E = math.log2(math.e)

LANES = 128
BF16_ROWS = 16
MLA_HEAD_PAD = 2 * LANES
VMEM_LIMIT_BYTES = 56 * 1024 * 1024


def _cparams(sem):
    return pltpu.CompilerParams(dimension_semantics=sem, vmem_limit_bytes=VMEM_LIMIT_BYTES)


def _resident(shape):
    return pl.BlockSpec(shape, lambda *_: (0, 0), pipeline_mode=pl.Buffered(1))


def _rms(x, g):
    ms = jnp.mean(x * x, axis=-1, keepdims=True)
    return x * lax.rsqrt(ms + NORM_EPS) * g


def _sigmoid(x):
    return 1.0 / (1.0 + jnp.exp(-x))


def _dot_nt(a, b):
    return lax.dot_general(a, b, (((1,), (1,)), ((), ())), preferred_element_type=F32)


def _chunk_rows(rows, nsteps):
    rc = BF16_ROWS
    while rows % rc or rows // rc > nsteps:
        rc += BF16_ROWS
    return rc


def _side_job_specs(casts, split_src, split, nsteps, lin):
    def spec(rows, cols):
        rc = _chunk_rows(rows, nsteps)
        last = rows // rc - 1
        return pl.BlockSpec((rc, cols), lambda *g: (jnp.minimum(lin(*g), last), 0))

    in_specs, out_specs, out_shape, operands = [], [], [], []
    for w in casts:
        in_specs.append(spec(*w.shape))
        out_specs.append(spec(*w.shape))
        out_shape.append(jax.ShapeDtypeStruct(w.shape, BF16))
        operands.append(w)
    if split_src is not None:
        rows, cols = split_src.shape
        keep, skip = split
        assert keep % skip == 0 and rows % skip == 0 and rows // skip <= nsteps and skip <= LANES
        last, hole = rows // skip - 1, keep // skip

        def packed_idx(*g):
            c = jnp.minimum(lin(*g), last)
            return (jnp.where(c <= hole, jnp.minimum(c, hole - 1), c - 1), 0)

        in_specs.append(pl.BlockSpec((skip, cols), lambda *g: (jnp.minimum(lin(*g), last), 0)))
        out_specs += [pl.BlockSpec((skip, cols), packed_idx),
                      pl.BlockSpec((LANES, cols), lambda *g: (0, 0))]
        out_shape += [jax.ShapeDtypeStruct((rows - skip, cols), BF16),
                      jax.ShapeDtypeStruct((LANES, cols), BF16)]
        operands.append(split_src)
    return in_specs, out_specs, out_shape, operands


def _run_side_jobs(in_refs, out_refs, split, step):
    n_cast = len(in_refs) - (1 if split else 0)
    for src, dst in zip(in_refs[:n_cast], out_refs[:n_cast]):
        dst[...] = src[...].astype(BF16)
    if split:
        keep, skip = split
        src = in_refs[n_cast]
        packed_ref, moved_ref = out_refs[n_cast:]
        hole = keep // skip

        @pl.when(step != hole)
        def _():
            packed_ref[...] = src[...].astype(BF16)

        @pl.when(step == hole)
        def _():
            moved_ref[:skip, :] = src[...].astype(BF16)
            if skip < LANES:
                moved_ref[skip:, :] = jnp.zeros((LANES - skip, src.shape[1]), BF16)


def _ffn_up_body(*refs, n_side, split):
    x_ref, nw_ref, wg_ref, wu_ref = refs[:4]
    side_in = refs[4:4 + n_side]
    a_ref = refs[4 + n_side]
    side_out = refs[5 + n_side:-1]
    xn_ref = refs[-1]
    j = pl.program_id(1)

    @pl.when(j == 0)
    def _():
        xn_ref[...] = _rms(x_ref[...], nw_ref[...]).astype(BF16)

    xn = xn_ref[...]
    g = jnp.dot(xn, wg_ref[...].astype(BF16), preferred_element_type=F32)
    u = jnp.dot(xn, wu_ref[...].astype(BF16), preferred_element_type=F32)
    a_ref[...] = (g * _sigmoid(g) * u * 0.5).astype(BF16)
    _run_side_jobs(side_in, side_out, split, pl.program_id(0) * pl.num_programs(1) + j)


def _ffn_down_body(a_ref, wd_ref, x_ref, o_ref):
    o_ref[...] = x_ref[...] + jnp.dot(a_ref[...], wd_ref[...], preferred_element_type=F32)


def _ffn(x, norm_w, wg, wu, wd, *, casts=(), split_src=None, split=None, tm=1024, tf=512, tn=512):
    t, d = x.shape
    f = wg.shape[1]
    nj = f // tf
    casts = list(casts)
    if wd.dtype != BF16:
        casts.append(wd)
    side = _side_job_specs(casts, split_src, split, (t // tm) * nj, lambda i, j: i * nj + j)
    outs = pl.pallas_call(
        functools.partial(_ffn_up_body, n_side=len(side[0]),
                          split=split if split_src is not None else None),
        grid=(t // tm, nj),
        in_specs=[
            pl.BlockSpec((tm, d), lambda i, j: (i, 0)),
            pl.BlockSpec((1, d), lambda i, j: (0, 0)),
            pl.BlockSpec((d, tf), lambda i, j: (0, j)),
            pl.BlockSpec((d, tf), lambda i, j: (0, j)),
        ] + side[0],
        out_specs=[pl.BlockSpec((tm, tf), lambda i, j: (i, j))] + side[1],
        out_shape=[jax.ShapeDtypeStruct((t, f), BF16)] + side[2],
        scratch_shapes=[pltpu.VMEM((tm, d), BF16)],
        compiler_params=_cparams(("arbitrary", "arbitrary")),
        name="ffn_up",
    )(x, norm_w, wg, wu, *side[3])
    converted = list(outs[1:])
    if wd.dtype != BF16:
        wd = converted.pop(len(casts) - 1)
    h = pl.pallas_call(
        _ffn_down_body,
        grid=(t // tm, d // tn),
        in_specs=[
            pl.BlockSpec((tm, f), lambda i, j: (i, 0)),
            pl.BlockSpec((f, tn), lambda i, j: (0, j)),
            pl.BlockSpec((tm, tn), lambda i, j: (i, j)),
        ],
        out_specs=pl.BlockSpec((tm, tn), lambda i, j: (i, j)),
        out_shape=jax.ShapeDtypeStruct((t, d), F32),
        compiler_params=_cparams(("parallel", "arbitrary")),
        name="ffn_down",
    )(outs[0], wd, x)
    return h, converted


def _in_proj_body(*refs, n_side):
    h_ref, nw_ref, w_ref, wkr_ref = refs[:4]
    side_in = refs[4:4 + n_side]
    z_ref, kr_ref = refs[4 + n_side:6 + n_side]
    side_out = refs[6 + n_side:-1]
    xn_ref = refs[-1]
    j = pl.program_id(1)

    @pl.when(j == 0)
    def _():
        xn = _rms(h_ref[...], nw_ref[...]).astype(BF16)
        xn_ref[...] = xn
        kr_ref[...] = _dot_nt(xn, wkr_ref[...])

    z_ref[...] = _dot_nt(xn_ref[...], w_ref[...]).astype(BF16)
    _run_side_jobs(side_in, side_out, None, None)


def _in_proj(h, norm_w, w_main_t, w_kr_t, *, casts=(), tm=1024, tn=1024):
    t, d = h.shape
    n = w_main_t.shape[0]
    nj = n // tn
    side = _side_job_specs(casts, None, None, (t // tm) * nj, lambda i, j: i * nj + j)
    outs = pl.pallas_call(
        functools.partial(_in_proj_body, n_side=len(side[0])),
        grid=(t // tm, nj),
        in_specs=[
            pl.BlockSpec((tm, d), lambda i, j: (i, 0)),
            pl.BlockSpec((1, d), lambda i, j: (0, 0)),
            pl.BlockSpec((tn, d), lambda i, j: (j, 0)),
            pl.BlockSpec((LANES, d), lambda i, j: (0, 0)),
        ] + side[0],
        out_specs=[
            pl.BlockSpec((tm, tn), lambda i, j: (i, j)),
            pl.BlockSpec((tm, LANES), lambda i, j: (i, 0)),
        ] + side[1],
        out_shape=[
            jax.ShapeDtypeStruct((t, n), BF16),
            jax.ShapeDtypeStruct((t, LANES), F32),
        ] + side[2],
        scratch_shapes=[pltpu.VMEM((tm, d), BF16)],
        compiler_params=_cparams(("arbitrary", "arbitrary")),
        name="in_proj",
    )(h, norm_w, w_main_t, w_kr_t, *side[3])
    return outs[0], outs[1], list(outs[2:])


def _rope(r, cos_t, sin_t):
    lane = lax.broadcasted_iota(jnp.int32, r.shape, 1)
    half = MLA_ROPE_DIM // 2
    swapped = jnp.where(lane < half, pltpu.roll(r, LANES - half, 1), pltpu.roll(r, half, 1))
    return r * cos_t + swapped * sin_t


def _mla_prep_body(ql_ref, kvl_ref, kr_ref, qn_ref, kvn_ref, wuq_ref, wukv_ref,
                   cos_ref, sin_ref, q_out, k_out, v_out):
    cq = _rms(ql_ref[...].astype(F32), qn_ref[...]).astype(BF16)
    ckv = _rms(kvl_ref[...].astype(F32), kvn_ref[...]).astype(BF16)
    cos_t = cos_ref[...]
    sin_t = sin_ref[...]
    k_rope = _rope(kr_ref[...], cos_t, sin_t).astype(BF16)
    for h in range(MLA_HEADS):
        lo = h * MLA_HEAD_PAD
        mid = lo + LANES
        hi = lo + MLA_HEAD_PAD
        q = jnp.dot(cq, wuq_ref[:, lo:hi], preferred_element_type=F32)
        kv = jnp.dot(ckv, wukv_ref[:, lo:hi], preferred_element_type=F32)
        q_out[:, lo:mid] = q[:, :LANES].astype(BF16)
        q_out[:, mid:hi] = _rope(q[:, LANES:], cos_t, sin_t).astype(BF16)
        k_out[:, lo:mid] = kv[:, :LANES].astype(BF16)
        k_out[:, mid:hi] = k_rope
        v_out[:, lo:mid] = kv[:, LANES:].astype(BF16)
        v_out[:, mid:hi] = jnp.ones((kv.shape[0], MLA_HEAD_PAD - MLA_V_DIM), BF16)


def _mla_prep(z, kr, q_norm, kv_norm, wuq_p, wukv, cos_t, sin_t, *, seq, ql_blk, kvl_blk, tm=512):
    t = z.shape[0]
    nseq = seq // tm
    qw = MLA_HEADS * MLA_HEAD_PAD
    return pl.pallas_call(
        _mla_prep_body,
        grid=(t // tm,),
        in_specs=[
            pl.BlockSpec((tm, MLA_Q_RANK), lambda i: (i, ql_blk)),
            pl.BlockSpec((tm, MLA_KV_RANK), lambda i: (i, kvl_blk)),
            pl.BlockSpec((tm, LANES), lambda i: (i, 0)),
            pl.BlockSpec((1, MLA_Q_RANK), lambda i: (0, 0)),
            pl.BlockSpec((1, MLA_KV_RANK), lambda i: (0, 0)),
            pl.BlockSpec((MLA_Q_RANK, qw), lambda i: (0, 0)),
            pl.BlockSpec((MLA_KV_RANK, qw), lambda i: (0, 0)),
            pl.BlockSpec((tm, LANES), lambda i: (i % nseq, 0)),
            pl.BlockSpec((tm, LANES), lambda i: (i % nseq, 0)),
        ],
        out_specs=[
            pl.BlockSpec((tm, qw), lambda i: (i, 0)),
            pl.BlockSpec((tm, qw), lambda i: (i, 0)),
            pl.BlockSpec((tm, qw), lambda i: (i, 0)),
        ],
        out_shape=[
            jax.ShapeDtypeStruct((t, qw), BF16),
            jax.ShapeDtypeStruct((t, qw), BF16),
            jax.ShapeDtypeStruct((t, qw), BF16),
        ],
        compiler_params=_cparams(("parallel",)),
        name="mla_prep",
    )(z, z, kr, q_norm, kv_norm, wuq_p, wukv, cos_t, sin_t)


RPB_ROWS = 2 * NA_WIN_ROWS - 1
RPB_COLS = 2 * NA_WIN_COLS - 1
NA_Q_ROWS = 4
NA_K_ROWS = NA_Q_ROWS + NA_WIN_ROWS
NA_VARIANTS = 3


def _na_variant_geometry(var, j):
    lo = (0, j, NA_K_ROWS - NA_WIN_ROWS)[var]
    rho0 = (NA_WIN_ROWS - 1) - j - (NA_WIN_ROWS // 2) * var
    return lo, rho0


def _na_bias_body(rpb_ref, o_ref, pair_ref):
    h = pl.program_id(0)
    kw = NA_WIN_COLS
    shape = (GRID_W, LANES)
    q = lax.broadcasted_iota(jnp.int32, shape, 0)
    lane = lax.broadcasted_iota(jnp.int32, shape, 1)
    first = lane < GRID_W
    k = jnp.where(first, lane, lane - GRID_W)
    col_start = jnp.clip(q - kw // 2, 0, GRID_W - kw)
    valid = (k >= col_start) & (k < col_start + kw)
    dc = jnp.clip(k - q, -(kw - 1), kw - 1) + (kw - 1)
    neg = jnp.full(shape, NEG_INF, F32)

    for e in range(RPB_ROWS + 1):
        base_a = (h * RPB_ROWS + max(e - 1, 0)) * RPB_COLS
        base_b = (h * RPB_ROWS + min(e, RPB_ROWS - 1)) * RPB_COLS
        acc = jnp.zeros(shape, F32)
        for c in range(RPB_COLS):
            val = jnp.where(first, rpb_ref[base_a + c], rpb_ref[base_b + c])
            acc = jnp.where(dc == c, val, acc)
        pair_ref[e] = jnp.where(valid, acc * LOG2E, NEG_INF)

    for var in range(NA_VARIANTS):
        for j in range(NA_Q_ROWS):
            lo, rho0 = _na_variant_geometry(var, j)
            for pair in range(NA_K_ROWS // 2):
                i = 2 * pair
                in_a = lo <= i < lo + NA_WIN_ROWS
                in_b = lo <= i + 1 < lo + NA_WIN_ROWS
                if in_a or in_b:
                    tile = pair_ref[rho0 + i + 1]
                    if not in_a:
                        tile = jnp.where(first, neg, tile)
                    if not in_b:
                        tile = jnp.where(first, tile, neg)
                else:
                    tile = neg
                o_ref[var, 0, j * GRID_W:(j + 1) * GRID_W, pair * LANES:(pair + 1) * LANES] = tile


def _na_bias(rpb_flat):
    nq = NA_Q_ROWS * GRID_W
    nk = NA_K_ROWS * GRID_W
    return pl.pallas_call(
        _na_bias_body,
        grid=(NA_HEADS,),
        in_specs=[pl.BlockSpec(memory_space=pltpu.SMEM)],
        out_specs=pl.BlockSpec((NA_VARIANTS, 1, nq, nk), lambda h: (0, h, 0, 0)),
        out_shape=jax.ShapeDtypeStruct((NA_VARIANTS, NA_HEADS, nq, nk), F32),
        scratch_shapes=[pltpu.VMEM((RPB_ROWS + 1, GRID_W, LANES), F32)],
        compiler_params=_cparams(("arbitrary",)),
        name="na_bias",
    )(rpb_flat)


def _na_key_start(g, rows):
    return jnp.clip(g * NA_Q_ROWS - NA_WIN_ROWS // 2, 0, rows - NA_K_ROWS)


def _na_body(q_ref, k_ref, v_ref, b_ref, o_ref, *, rows):
    g = pl.program_id(1)
    start = pl.multiple_of(_na_key_start(g, rows) * GRID_W, GRID_W)
    kind = jnp.where(g == 0, 0, jnp.where(g == rows // NA_Q_ROWS - 1, 2, 1))
    nkeys = NA_K_ROWS * GRID_W
    c = (NA_HEAD_DIM ** -0.5) * LOG2E
    ones = jnp.ones((nkeys, NA_HEAD_DIM), BF16)
    for h in range(NA_HEADS):
        cs = slice(h * NA_HEAD_DIM, (h + 1) * NA_HEAD_DIM)
        q = q_ref[0, :, cs]
        k = k_ref[0, pl.ds(start, nkeys), cs]
        v = v_ref[0, pl.ds(start, nkeys), cs]
        s = _dot_nt(q, k)
        t = s * c + b_ref[kind, h]
        m = jnp.max(t, axis=-1, keepdims=True)
        p = jnp.exp2(t - m).astype(BF16)
        oe = jnp.dot(p, jnp.concatenate([v, ones], axis=1), preferred_element_type=F32)
        o_ref[0, :, cs] = (oe[:, :NA_HEAD_DIM] / oe[:, NA_HEAD_DIM:]).astype(BF16)


def _na_attn(z3, bias, *, q_blk, k_blk, v_blk):
    b, s, _ = z3.shape
    rows = s // GRID_W
    assert rows % NA_Q_ROWS == 0 and rows >= 2 * NA_K_ROWS - NA_WIN_ROWS
    groups = rows // NA_Q_ROWS
    nq = NA_Q_ROWS * GRID_W
    return pl.pallas_call(
        functools.partial(_na_body, rows=rows),
        grid=(b, groups),
        in_specs=[
            pl.BlockSpec((1, nq, NA_WIDTH), lambda bi, g: (bi, g, q_blk)),
            pl.BlockSpec((1, s, NA_WIDTH), lambda bi, g: (bi, 0, k_blk)),
            pl.BlockSpec((1, s, NA_WIDTH), lambda bi, g: (bi, 0, v_blk)),
            pl.BlockSpec(bias.shape, lambda bi, g: (0, 0, 0, 0), pipeline_mode=pl.Buffered(1)),
        ],
        out_specs=pl.BlockSpec((1, nq, NA_WIDTH), lambda bi, g: (bi, g, 0)),
        out_shape=jax.ShapeDtypeStruct((b, s, NA_WIDTH), BF16),
        compiler_params=_cparams(("parallel", "arbitrary")),
        name="na_attn",
    )(z3, z3, z3, bias)


def _mla_body(q_ref, k_ref, v_ref, o_ref, *, tq, tk):
    seq = q_ref.shape[1]
    c = (MLA_QK_DIM ** -0.5) * LOG2E
    for qi in range(seq // tq):
        q = q_ref[0, qi * tq:(qi + 1) * tq, :]
        m = acc = None
        for kj in range(seq // tk):
            ks = slice(kj * tk, (kj + 1) * tk)
            s = _dot_nt(q, k_ref[0, ks, :]) * c
            mj = jnp.max(s, axis=-1, keepdims=True)
            m_new = mj if m is None else jnp.maximum(m, mj)
            p = jnp.exp2(s - m_new).astype(BF16)
            pv = jnp.dot(p, v_ref[0, ks, :], preferred_element_type=F32)
            acc = pv if acc is None else acc * jnp.exp2(m - m_new) + pv
            m = m_new
        o_ref[0, qi * tq:(qi + 1) * tq, :] = (acc[:, :MLA_V_DIM] / acc[:, MLA_V_DIM:]).astype(BF16)


def _mla_attn(q3, k3, v3, *, tq=512, tk=512):
    b, s, _ = q3.shape
    vw = v3.shape[2] // MLA_HEADS
    return pl.pallas_call(
        functools.partial(_mla_body, tq=tq, tk=tk),
        grid=(b, MLA_HEADS),
        in_specs=[
            pl.BlockSpec((1, s, MLA_HEAD_PAD), lambda bi, h: (bi, 0, h)),
            pl.BlockSpec((1, s, MLA_HEAD_PAD), lambda bi, h: (bi, 0, h)),
            pl.BlockSpec((1, s, vw), lambda bi, h: (bi, 0, h)),
        ],
        out_specs=pl.BlockSpec((1, s, MLA_V_DIM), lambda bi, h: (bi, 0, h)),
        out_shape=jax.ShapeDtypeStruct((b, s, MLA_WIDTH), BF16),
        compiler_params=_cparams(("parallel", "arbitrary")),
        name="mla_attn",
    )(q3, k3, v3)


def _merge_body(na_ref, mla_ref, ga_ref, gb_ref, h_ref, wa_ref, wb_ref, wo_ref, o_ref):
    ya = jnp.dot(na_ref[...], wa_ref[...], preferred_element_type=F32)
    yb = jnp.dot(mla_ref[...], wb_ref[...], preferred_element_type=F32)
    merged = _sigmoid(ga_ref[...].astype(F32)) * ya + _sigmoid(gb_ref[...].astype(F32)) * yb
    o_ref[...] = h_ref[...] + jnp.dot(merged.astype(BF16), wo_ref[...],
                                      preferred_element_type=F32)


def _merge(na_o, mla_o, z, h, wa, wb, wo, *, ga_blk, gb_blk, tm=256):
    t, d = h.shape
    return pl.pallas_call(
        _merge_body,
        grid=(t // tm,),
        in_specs=[
            pl.BlockSpec((tm, NA_WIDTH), lambda i: (i, 0)),
            pl.BlockSpec((tm, MLA_WIDTH), lambda i: (i, 0)),
            pl.BlockSpec((tm, d), lambda i: (i, ga_blk)),
            pl.BlockSpec((tm, d), lambda i: (i, gb_blk)),
            pl.BlockSpec((tm, d), lambda i: (i, 0)),
            _resident((NA_WIDTH, d)),
            _resident((MLA_WIDTH, d)),
            _resident((d, d)),
        ],
        out_specs=pl.BlockSpec((tm, d), lambda i: (i, 0)),
        out_shape=jax.ShapeDtypeStruct((t, d), F32),
        compiler_params=_cparams(("parallel",)),
        name="merge",
    )(na_o, mla_o, z, z, h, wa, wb, wo)


def _pl_body(h_ref, p_ref, nw_ref, wg_ref, wp_ref, fn_ref, o_ref, *, final):
    h = h_ref[...]
    xn = _rms(h, nw_ref[...]).astype(BF16)
    gate = _sigmoid(jnp.dot(xn, wg_ref[...], preferred_element_type=F32))
    emb = jnp.dot(p_ref[...].astype(BF16), wp_ref[...], preferred_element_type=F32)
    h = h + gate * emb
    o_ref[...] = _rms(h, fn_ref[...]) if final else h


def _pl_embed(h, p, norm_w, w_gate, w_pl, final_norm, *, final, tm=512):
    t, d = h.shape
    pdim = p.shape[1]
    return pl.pallas_call(
        functools.partial(_pl_body, final=final),
        grid=(t // tm,),
        in_specs=[
            pl.BlockSpec((tm, d), lambda i: (i, 0)),
            pl.BlockSpec((tm, pdim), lambda i: (i, 0)),
            _resident((1, d)),
            _resident((d, d)),
            _resident((pdim, d)),
            _resident((1, d)),
        ],
        out_specs=pl.BlockSpec((tm, d), lambda i: (i, 0)),
        out_shape=jax.ShapeDtypeStruct((t, d), F32),
        compiler_params=_cparams(("parallel",)),
        name="pl_final" if final else "pl_embed",
    )(h, p, norm_w, w_gate, w_pl, final_norm)


def _final_norm_body(h_ref, fn_ref, o_ref):
    o_ref[...] = _rms(h_ref[...], fn_ref[...])


def _rope_tables(seq):
    half = MLA_ROPE_DIM // 2
    pos = jnp.arange(seq, dtype=F32)
    inv_freq = 1.0 / (ROPE_THETA ** (jnp.arange(0, MLA_ROPE_DIM, 2, dtype=F32) / MLA_ROPE_DIM))
    ang = pos[:, None] * inv_freq[None, :]
    cos, sin = jnp.cos(ang), jnp.sin(ang)
    zeros = jnp.zeros((seq, LANES - 2 * half), F32)
    return (jnp.concatenate([cos, cos, zeros], axis=-1),
            jnp.concatenate([-sin, sin, zeros], axis=-1))


def kernel(x, p, ffn1_norm, ffn1_w_gate, ffn1_w_up, ffn1_w_down, mix_norm, w_in, q_a_norm, w_uq, kv_a_norm, w_ukv, na_rpb, w_branch_a, w_branch_b, w_out, ffn2_norm, ffn2_w_gate, ffn2_w_up, ffn2_w_down, pl_norm, w_pl, w_pl_gate, final_norm):
    bsz, seq, d = x.shape
    depth = p.shape[0]
    t = bsz * seq
    bf = lambda w: w.astype(BF16)
    row = lambda g: g.reshape(1, -1).astype(F32)

    c_lat = 3 * NA_WIDTH
    c_rope = c_lat + MLA_Q_RANK + MLA_KV_RANK
    cos_t, sin_t = _rope_tables(seq)

    h = x.reshape(t, d)
    for i in range(depth):
        h, (w_main_t, w_kr_t) = _ffn(
            h, row(ffn1_norm[i]), ffn1_w_gate[i], ffn1_w_up[i], ffn1_w_down[i],
            split_src=jnp.swapaxes(w_in[i], 0, 1), split=(c_rope, MLA_ROPE_DIM))
        later = [ffn2_w_gate[i], ffn2_w_up[i], ffn2_w_down[i], w_branch_a[i], w_branch_b[i],
                 w_out[i], w_pl_gate[i]]
        z, kr, (w2_gate, w2_up, w2_down, wa, wb, wo, w_plg) = _in_proj(
            h, row(mix_norm[i]), w_main_t, w_kr_t, casts=later)

        wuq = w_uq[i].reshape(MLA_Q_RANK, MLA_HEADS, MLA_QK_DIM)
        wuq_p = bf(jnp.pad(wuq, ((0, 0), (0, 0), (0, MLA_HEAD_PAD - MLA_QK_DIM)))
                   .reshape(MLA_Q_RANK, MLA_HEADS * MLA_HEAD_PAD))
        q_m, k_m, v_m = _mla_prep(
            z, kr, row(q_a_norm[i]), row(kv_a_norm[i]), wuq_p, bf(w_ukv[i]), cos_t, sin_t,
            seq=seq, ql_blk=c_lat // MLA_Q_RANK, kvl_blk=(c_lat + MLA_Q_RANK) // MLA_KV_RANK)

        bias = _na_bias(na_rpb[i].reshape(-1).astype(F32))
        z3 = z.reshape(bsz, seq, z.shape[1])
        na_o = _na_attn(z3, bias, q_blk=0, k_blk=1, v_blk=2)
        mla_o = _mla_attn(q_m.reshape(bsz, seq, -1), k_m.reshape(bsz, seq, -1),
                          v_m.reshape(bsz, seq, -1))

        gate_col = c_rope
        h = _merge(na_o.reshape(t, NA_WIDTH), mla_o.reshape(t, MLA_WIDTH), z, h, wa, wb, wo,
                   ga_blk=gate_col // d, gb_blk=gate_col // d + 1)

        h, _ = _ffn(h, row(ffn2_norm[i]), w2_gate, w2_up, w2_down)

        h = _pl_embed(h, p[i].reshape(t, -1), row(pl_norm[i]), w_plg, bf(w_pl[i]),
                      row(final_norm), final=(i == depth - 1))

    if depth == 0:
        tm = 512
        h = pl.pallas_call(
            _final_norm_body,
            grid=(t // tm,),
            in_specs=[pl.BlockSpec((tm, d), lambda i: (i, 0)),
                      pl.BlockSpec((1, d), lambda i: (0, 0))],
            out_specs=pl.BlockSpec((tm, d), lambda i: (i, 0)),
            out_shape=jax.ShapeDtypeStruct((t, d), F32),
            compiler_params=_cparams(("parallel",)),
            name="final_norm",
        )(h, row(final_norm))
    return h.reshape(bsz, seq, d)
```

```python
import functools
import math

import jax
import jax.numpy as jnp
from jax import lax
from jax.experimental import pallas as pl
from jax.experimental.pallas import tpu as pltpu

F32 = jnp.float32
BF16 = jnp.bfloat16

GRID_W = 64
NA_HEADS = 8
NA_HEAD_DIM = 128
NA_WIN_ROWS = 8
NA_WIN_COLS = 16
NA_WIDTH = NA_HEADS * NA_HEAD_DIM
MLA_HEADS = 8
MLA_Q_RANK = 512
MLA_KV_RANK = 512
MLA_NOPE_DIM = 128
MLA_ROPE_DIM = 64
MLA_V_DIM = 128
MLA_QK_DIM = MLA_NOPE_DIM + MLA_ROPE_DIM
MLA_WIDTH = MLA_HEADS * MLA_V_DIM
ROPE_THETA = 10000.0
NORM_EPS = 1e-6
NEG_INF = -1e30
LOG2E = math.log2(math.e)

LANES = 128
BF16_ROWS = 16
MLA_HEAD_PAD = 2 * LANES
VMEM_LIMIT_BYTES = 56 * 1024 * 1024


def _cparams(sem):
    return pltpu.CompilerParams(dimension_semantics=sem, vmem_limit_bytes=VMEM_LIMIT_BYTES)


def _resident(shape):
    return pl.BlockSpec(shape, lambda *_: (0, 0), pipeline_mode=pl.Buffered(1))


def _rms(x, g):
    ms = jnp.mean(x * x, axis=-1, keepdims=True)
    return x * lax.rsqrt(ms + NORM_EPS) * g


def _sigmoid(x):
    return 1.0 / (1.0 + jnp.exp(-x))


def _dot_nt(a, b):
    return lax.dot_general(a, b, (((1,), (1,)), ((), ())), preferred_element_type=F32)


def _chunk_rows(rows, nsteps):
    rc = BF16_ROWS
    while rows % rc or rows // rc > nsteps:
        rc += BF16_ROWS
    return rc


def _side_job_specs(casts, split_src, split, nsteps, lin):
    def spec(rows, cols):
        rc = _chunk_rows(rows, nsteps)
        last = rows // rc - 1
        return pl.BlockSpec((rc, cols), lambda *g: (jnp.minimum(lin(*g), last), 0))

    in_specs, out_specs, out_shape, operands = [], [], [], []
    for w in casts:
        in_specs.append(spec(*w.shape))
        out_specs.append(spec(*w.shape))
        out_shape.append(jax.ShapeDtypeStruct(w.shape, BF16))
        operands.append(w)
    if split_src is not None:
        rows, cols = split_src.shape
        keep, skip = split
        assert keep % skip == 0 and rows % skip == 0 and rows // skip <= nsteps and skip <= LANES
        last, hole = rows // skip - 1, keep // skip

        def packed_idx(*g):
            c = jnp.minimum(lin(*g), last)
            return (jnp.where(c <= hole, jnp.minimum(c, hole - 1), c - 1), 0)

        in_specs.append(pl.BlockSpec((skip, cols), lambda *g: (jnp.minimum(lin(*g), last), 0)))
        out_specs += [pl.BlockSpec((skip, cols), packed_idx),
                      pl.BlockSpec((LANES, cols), lambda *g: (0, 0))]
        out_shape += [jax.ShapeDtypeStruct((rows - skip, cols), BF16),
                      jax.ShapeDtypeStruct((LANES, cols), BF16)]
        operands.append(split_src)
    return in_specs, out_specs, out_shape, operands


def _run_side_jobs(in_refs, out_refs, split, step):
    n_cast = len(in_refs) - (1 if split else 0)
    for src, dst in zip(in_refs[:n_cast], out_refs[:n_cast]):
        dst[...] = src[...].astype(BF16)
    if split:
        keep, skip = split
        src = in_refs[n_cast]
        packed_ref, moved_ref = out_refs[n_cast:]
        hole = keep // skip

        @pl.when(step != hole)
        def _():
            packed_ref[...] = src[...].astype(BF16)

        @pl.when(step == hole)
        def _():
            moved_ref[:skip, :] = src[...].astype(BF16)
            if skip < LANES:
                moved_ref[skip:, :] = jnp.zeros((LANES - skip, src.shape[1]), BF16)


def _ffn_up_body(*refs, n_side, split, emit_weights, aliased):
    x_ref, nw_ref, wg_ref, wu_ref = refs[:4]
    n_in = 4 + int(aliased) + n_side
    side_in = refs[n_in - n_side:n_in]
    a_ref = refs[n_in]
    side_out = refs[n_in + 1 + 2 * int(emit_weights):-1]
    xn_ref = refs[-1]
    j = pl.program_id(1)

    @pl.when(j == 0)
    def _():
        xn_ref[...] = _rms(x_ref[...], nw_ref[...]).astype(BF16)

    xn = xn_ref[...]
    wg = wg_ref[...].astype(BF16)
    wu = wu_ref[...].astype(BF16)
    if emit_weights:
        refs[n_in + 1][...] = wg
        refs[n_in + 2][...] = wu
    g = jnp.dot(xn, wg, preferred_element_type=F32)
    u = jnp.dot(xn, wu, preferred_element_type=F32)
    a_ref[...] = (g * _sigmoid(g) * u * 0.5).astype(BF16)
    _run_side_jobs(side_in, side_out, split, pl.program_id(0) * pl.num_programs(1) + j)


def _ffn_up(x, norm_w, wg, wu, *, tile0, ntiles, a_prev=None, casts=(), split_src=None, split=None,
            tm, tf):
    t, d = x.shape
    f = wg.shape[1]
    nj = f // tf
    emit = wg.dtype != BF16
    side = _side_job_specs(casts, split_src, split, ntiles * nj, lambda i, j: i * nj + j)
    wspec = pl.BlockSpec((d, tf), lambda i, j: (0, j))
    outs = pl.pallas_call(
        functools.partial(_ffn_up_body, n_side=len(side[0]), emit_weights=emit,
                          aliased=a_prev is not None,
                          split=split if split_src is not None else None),
        grid=(ntiles, nj),
        in_specs=[
            pl.BlockSpec((tm, d), lambda i, j: (i + tile0, 0)),
            pl.BlockSpec((1, d), lambda i, j: (0, 0)),
            wspec,
            wspec,
        ] + ([pl.BlockSpec(memory_space=pl.ANY)] if a_prev is not None else []) + side[0],
        out_specs=[pl.BlockSpec((tm, tf), lambda i, j: (i + tile0, j))]
        + ([wspec, wspec] if emit else []) + side[1],
        out_shape=[jax.ShapeDtypeStruct((t, f), BF16)]
        + ([jax.ShapeDtypeStruct(wg.shape, BF16)] * 2 if emit else []) + side[2],
        scratch_shapes=[pltpu.VMEM((tm, d), BF16)],
        input_output_aliases={4: 0} if a_prev is not None else {},
        compiler_params=_cparams(("arbitrary", "arbitrary")),
        name="ffn_up",
    )(x, norm_w, wg, wu, *([a_prev] if a_prev is not None else []), *side[3])
    n_w = 2 if emit else 0
    return outs[0], list(outs[1:1 + n_w]), list(outs[1 + n_w:])


def _ffn_down_body(a_ref, wd_ref, x_ref, o_ref):
    o_ref[...] = x_ref[...] + jnp.dot(a_ref[...], wd_ref[...], preferred_element_type=F32)


def _ffn(x, norm_w, wg, wu, wd, *, casts=(), split_src=None, split=None, tm=1024, tf=512, tn=512):
    t, d = x.shape
    f = wg.shape[1]
    ntiles = t // tm
    casts = list(casts)
    if wd.dtype != BF16:
        casts.append(wd)
    a, tile0 = None, 0
    if wg.dtype != BF16 and ntiles > 1:
        a, (wg, wu), _ = _ffn_up(x, norm_w, wg, wu, tile0=0, ntiles=1, tm=tm, tf=tf)
        tile0 = 1
    a, _, converted = _ffn_up(x, norm_w, wg, wu, tile0=tile0, ntiles=ntiles - tile0, a_prev=a,
                              casts=casts, split_src=split_src, split=split, tm=tm, tf=tf)
    if wd.dtype != BF16:
        wd = converted.pop(len(casts) - 1)
    h = pl.pallas_call(
        _ffn_down_body,
        grid=(t // tm, d // tn),
        in_specs=[
            pl.BlockSpec((tm, f), lambda i, j: (i, 0)),
            pl.BlockSpec((f, tn), lambda i, j: (0, j)),
            pl.BlockSpec((tm, tn), lambda i, j: (i, j)),
        ],
        out_specs=pl.BlockSpec((tm, tn), lambda i, j: (i, j)),
        out_shape=jax.ShapeDtypeStruct((t, d), F32),
        compiler_params=_cparams(("parallel", "arbitrary")),
        name="ffn_down",
    )(a, wd, x)
    return h, converted


def _in_proj_body(*refs, n_side):
    h_ref, nw_ref, w_ref, wkr_ref = refs[:4]
    side_in = refs[4:4 + n_side]
    z_ref, kr_ref = refs[4 + n_side:6 + n_side]
    side_out = refs[6 + n_side:-1]
    xn_ref = refs[-1]
    j = pl.program_id(1)

    @pl.when(j == 0)
    def _():
        xn = _rms(h_ref[...], nw_ref[...]).astype(BF16)
        xn_ref[...] = xn
        kr_ref[...] = _dot_nt(xn, wkr_ref[...])

    z_ref[...] = _dot_nt(xn_ref[...], w_ref[...]).astype(BF16)
    _run_side_jobs(side_in, side_out, None, None)


def _in_proj(h, norm_w, w_main_t, w_kr_t, *, casts=(), tm=1024, tn=1024):
    t, d = h.shape
    n = w_main_t.shape[0]
    nj = n // tn
    side = _side_job_specs(casts, None, None, (t // tm) * nj, lambda i, j: i * nj + j)
    outs = pl.pallas_call(
        functools.partial(_in_proj_body, n_side=len(side[0])),
        grid=(t // tm, nj),
        in_specs=[
            pl.BlockSpec((tm, d), lambda i, j: (i, 0)),
            pl.BlockSpec((1, d), lambda i, j: (0, 0)),
            pl.BlockSpec((tn, d), lambda i, j: (j, 0)),
            pl.BlockSpec((LANES, d), lambda i, j: (0, 0)),
        ] + side[0],
        out_specs=[
            pl.BlockSpec((tm, tn), lambda i, j: (i, j)),
            pl.BlockSpec((tm, LANES), lambda i, j: (i, 0)),
        ] + side[1],
        out_shape=[
            jax.ShapeDtypeStruct((t, n), BF16),
            jax.ShapeDtypeStruct((t, LANES), F32),
        ] + side[2],
        scratch_shapes=[pltpu.VMEM((tm, d), BF16)],
        compiler_params=_cparams(("arbitrary", "arbitrary")),
        name="in_proj",
    )(h, norm_w, w_main_t, w_kr_t, *side[3])
    return outs[0], outs[1], list(outs[2:])


def _rope(r, cos_t, sin_t):
    lane = lax.broadcasted_iota(jnp.int32, r.shape, 1)
    half = MLA_ROPE_DIM // 2
    swapped = jnp.where(lane < half, pltpu.roll(r, LANES - half, 1), pltpu.roll(r, half, 1))
    return r * cos_t + swapped * sin_t


def _mla_prep_body(ql_ref, kvl_ref, kr_ref, qn_ref, kvn_ref, wuq_ref, wukv_ref,
                   cos_ref, sin_ref, q_out, k_out, v_out):
    cq = _rms(ql_ref[...].astype(F32), qn_ref[...]).astype(BF16)
    ckv = _rms(kvl_ref[...].astype(F32), kvn_ref[...]).astype(BF16)
    cos_t = cos_ref[...]
    sin_t = sin_ref[...]
    k_rope = _rope(kr_ref[...], cos_t, sin_t).astype(BF16)
    for h in range(MLA_HEADS):
        lo = h * MLA_HEAD_PAD
        mid = lo + LANES
        hi = lo + MLA_HEAD_PAD
        q = jnp.dot(cq, wuq_ref[:, lo:hi], preferred_element_type=F32)
        kv = jnp.dot(ckv, wukv_ref[:, lo:hi], preferred_element_type=F32)
        q_out[:, lo:mid] = q[:, :LANES].astype(BF16)
        q_out[:, mid:hi] = _rope(q[:, LANES:], cos_t, sin_t).astype(BF16)
        k_out[:, lo:mid] = kv[:, :LANES].astype(BF16)
        k_out[:, mid:hi] = k_rope
        v_out[:, lo:mid] = kv[:, LANES:].astype(BF16)
        v_out[:, mid:hi] = jnp.ones((kv.shape[0], MLA_HEAD_PAD - MLA_V_DIM), BF16)


def _mla_prep(z, kr, q_norm, kv_norm, wuq_p, wukv, cos_t, sin_t, *, seq, ql_blk, kvl_blk, tm=512):
    t = z.shape[0]
    nseq = seq // tm
    qw = MLA_HEADS * MLA_HEAD_PAD
    return pl.pallas_call(
        _mla_prep_body,
        grid=(t // tm,),
        in_specs=[
            pl.BlockSpec((tm, MLA_Q_RANK), lambda i: (i, ql_blk)),
            pl.BlockSpec((tm, MLA_KV_RANK), lambda i: (i, kvl_blk)),
            pl.BlockSpec((tm, LANES), lambda i: (i, 0)),
            pl.BlockSpec((1, MLA_Q_RANK), lambda i: (0, 0)),
            pl.BlockSpec((1, MLA_KV_RANK), lambda i: (0, 0)),
            pl.BlockSpec((MLA_Q_RANK, qw), lambda i: (0, 0)),
            pl.BlockSpec((MLA_KV_RANK, qw), lambda i: (0, 0)),
            pl.BlockSpec((tm, LANES), lambda i: (i % nseq, 0)),
            pl.BlockSpec((tm, LANES), lambda i: (i % nseq, 0)),
        ],
        out_specs=[
            pl.BlockSpec((tm, qw), lambda i: (i, 0)),
            pl.BlockSpec((tm, qw), lambda i: (i, 0)),
            pl.BlockSpec((tm, qw), lambda i: (i, 0)),
        ],
        out_shape=[
            jax.ShapeDtypeStruct((t, qw), BF16),
            jax.ShapeDtypeStruct((t, qw), BF16),
            jax.ShapeDtypeStruct((t, qw), BF16),
        ],
        compiler_params=_cparams(("parallel",)),
        name="mla_prep",
    )(z, z, kr, q_norm, kv_norm, wuq_p, wukv, cos_t, sin_t)


RPB_ROWS = 2 * NA_WIN_ROWS - 1
RPB_COLS = 2 * NA_WIN_COLS - 1
NA_Q_ROWS = 4
NA_K_ROWS = NA_Q_ROWS + NA_WIN_ROWS
NA_VARIANTS = 3


def _na_variant_geometry(var, j):
    lo = (0, j, NA_K_ROWS - NA_WIN_ROWS)[var]
    rho0 = (NA_WIN_ROWS - 1) - j - (NA_WIN_ROWS // 2) * var
    return lo, rho0


def _na_bias_body(rpb_ref, o_ref, pair_ref):
    h = pl.program_id(0)
    kw = NA_WIN_COLS
    shape = (GRID_W, LANES)
    q = lax.broadcasted_iota(jnp.int32, shape, 0)
    lane = lax.broadcasted_iota(jnp.int32, shape, 1)
    first = lane < GRID_W
    k = jnp.where(first, lane, lane - GRID_W)
    col_start = jnp.clip(q - kw // 2, 0, GRID_W - kw)
    valid = (k >= col_start) & (k < col_start + kw)
    dc = jnp.clip(k - q, -(kw - 1), kw - 1) + (kw - 1)
    neg = jnp.full(shape, NEG_INF, F32)

    for e in range(RPB_ROWS + 1):
        base_a = (h * RPB_ROWS + max(e - 1, 0)) * RPB_COLS
        base_b = (h * RPB_ROWS + min(e, RPB_ROWS - 1)) * RPB_COLS
        acc = jnp.zeros(shape, F32)
        for c in range(RPB_COLS):
            val = jnp.where(first, rpb_ref[base_a + c], rpb_ref[base_b + c])
            acc = jnp.where(dc == c, val, acc)
        pair_ref[e] = jnp.where(valid, acc * LOG2E, NEG_INF)

    for var in range(NA_VARIANTS):
        for j in range(NA_Q_ROWS):
            lo, rho0 = _na_variant_geometry(var, j)
            for pair in range(NA_K_ROWS // 2):
                i = 2 * pair
                in_a = lo <= i < lo + NA_WIN_ROWS
                in_b = lo <= i + 1 < lo + NA_WIN_ROWS
                if in_a or in_b:
                    tile = pair_ref[rho0 + i + 1]
                    if not in_a:
                        tile = jnp.where(first, neg, tile)
                    if not in_b:
                        tile = jnp.where(first, tile, neg)
                else:
                    tile = neg
                o_ref[var, 0, j * GRID_W:(j + 1) * GRID_W, pair * LANES:(pair + 1) * LANES] = tile


def _na_bias(rpb_flat):
    nq = NA_Q_ROWS * GRID_W
    nk = NA_K_ROWS * GRID_W
    return pl.pallas_call(
        _na_bias_body,
        grid=(NA_HEADS,),
        in_specs=[pl.BlockSpec(memory_space=pltpu.SMEM)],
        out_specs=pl.BlockSpec((NA_VARIANTS, 1, nq, nk), lambda h: (0, h, 0, 0)),
        out_shape=jax.ShapeDtypeStruct((NA_VARIANTS, NA_HEADS, nq, nk), F32),
        scratch_shapes=[pltpu.VMEM((RPB_ROWS + 1, GRID_W, LANES), F32)],
        compiler_params=_cparams(("arbitrary",)),
        name="na_bias",
    )(rpb_flat)


def _na_key_start(g, rows):
    return jnp.clip(g * NA_Q_ROWS - NA_WIN_ROWS // 2, 0, rows - NA_K_ROWS)


def _na_body(q_ref, k_ref, v_ref, b_ref, o_ref, *, rows):
    g = pl.program_id(1)
    start = pl.multiple_of(_na_key_start(g, rows) * GRID_W, GRID_W)
    kind = jnp.where(g == 0, 0, jnp.where(g == rows // NA_Q_ROWS - 1, 2, 1))
    nkeys = NA_K_ROWS * GRID_W
    c = (NA_HEAD_DIM ** -0.5) * LOG2E
    ones = jnp.ones((nkeys, NA_HEAD_DIM), BF16)
    for h in range(NA_HEADS):
        cs = slice(h * NA_HEAD_DIM, (h + 1) * NA_HEAD_DIM)
        q = q_ref[0, :, cs]
        k = k_ref[0, pl.ds(start, nkeys), cs]
        v = v_ref[0, pl.ds(start, nkeys), cs]
        s = _dot_nt(q, k)
        t = s * c + b_ref[kind, h]
        m = jnp.max(t, axis=-1, keepdims=True)
        p = jnp.exp2(t - m).astype(BF16)
        oe = jnp.dot(p, jnp.concatenate([v, ones], axis=1), preferred_element_type=F32)
        o_ref[0, :, cs] = (oe[:, :NA_HEAD_DIM] / oe[:, NA_HEAD_DIM:]).astype(BF16)


def _na_attn(z3, bias, *, q_blk, k_blk, v_blk):
    b, s, _ = z3.shape
    rows = s // GRID_W
    assert rows % NA_Q_ROWS == 0 and rows >= 2 * NA_K_ROWS - NA_WIN_ROWS
    groups = rows // NA_Q_ROWS
    nq = NA_Q_ROWS * GRID_W
    return pl.pallas_call(
        functools.partial(_na_body, rows=rows),
        grid=(b, groups),
        in_specs=[
            pl.BlockSpec((1, nq, NA_WIDTH), lambda bi, g: (bi, g, q_blk)),
            pl.BlockSpec((1, s, NA_WIDTH), lambda bi, g: (bi, 0, k_blk)),
            pl.BlockSpec((1, s, NA_WIDTH), lambda bi, g: (bi, 0, v_blk)),
            pl.BlockSpec(bias.shape, lambda bi, g: (0, 0, 0, 0), pipeline_mode=pl.Buffered(1)),
        ],
        out_specs=pl.BlockSpec((1, nq, NA_WIDTH), lambda bi, g: (bi, g, 0)),
        out_shape=jax.ShapeDtypeStruct((b, s, NA_WIDTH), BF16),
        compiler_params=_cparams(("parallel", "arbitrary")),
        name="na_attn",
    )(z3, z3, z3, bias)


def _mla_body(*refs, n_side, tq, tk):
    q_ref, k_ref, v_ref = refs[:3]
    o_ref = refs[3 + n_side]
    _run_side_jobs(refs[3:3 + n_side], refs[4 + n_side:], None, None)
    seq = q_ref.shape[1]
    c = (MLA_QK_DIM ** -0.5) * LOG2E
    for qi in range(seq // tq):
        q = q_ref[0, qi * tq:(qi + 1) * tq, :]
        m = acc = None
        for kj in range(seq // tk):
            ks = slice(kj * tk, (kj + 1) * tk)
            s = _dot_nt(q, k_ref[0, ks, :]) * c
            mj = jnp.max(s, axis=-1, keepdims=True)
            m_new = mj if m is None else jnp.maximum(m, mj)
            p = jnp.exp2(s - m_new).astype(BF16)
            pv = jnp.dot(p, v_ref[0, ks, :], preferred_element_type=F32)
            acc = pv if acc is None else acc * jnp.exp2(m - m_new) + pv
            m = m_new
        o_ref[0, qi * tq:(qi + 1) * tq, :] = (acc[:, :MLA_V_DIM] / acc[:, MLA_V_DIM:]).astype(BF16)


def _mla_attn(q3, k3, v3, *, casts=(), tq=512, tk=512):
    b, s, _ = q3.shape
    vw = v3.shape[2] // MLA_HEADS
    side = _side_job_specs(casts, None, None, b * MLA_HEADS, lambda bi, h: bi * MLA_HEADS + h)
    outs = pl.pallas_call(
        functools.partial(_mla_body, n_side=len(side[0]), tq=tq, tk=tk),
        grid=(b, MLA_HEADS),
        in_specs=[
            pl.BlockSpec((1, s, MLA_HEAD_PAD), lambda bi, h: (bi, 0, h)),
            pl.BlockSpec((1, s, MLA_HEAD_PAD), lambda bi, h: (bi, 0, h)),
            pl.BlockSpec((1, s, vw), lambda bi, h: (bi, 0, h)),
        ] + side[0],
        out_specs=[pl.BlockSpec((1, s, MLA_V_DIM), lambda bi, h: (bi, 0, h))] + side[1],
        out_shape=[jax.ShapeDtypeStruct((b, s, MLA_WIDTH), BF16)] + side[2],
        compiler_params=_cparams(("arbitrary", "arbitrary")),
        name="mla_attn",
    )(q3, k3, v3, *side[3])
    return outs[0], list(outs[1:])


def _merge_body(na_ref, mla_ref, ga_ref, gb_ref, h_ref, wa_ref, wb_ref, wo_ref, o_ref):
    ya = jnp.dot(na_ref[...], wa_ref[...], preferred_element_type=F32)
    yb = jnp.dot(mla_ref[...], wb_ref[...], preferred_element_type=F32)
    merged = _sigmoid(ga_ref[...].astype(F32)) * ya + _sigmoid(gb_ref[...].astype(F32)) * yb
    o_ref[...] = h_ref[...] + jnp.dot(merged.astype(BF16), wo_ref[...],
                                      preferred_element_type=F32)


def _merge(na_o, mla_o, z, h, wa, wb, wo, *, ga_blk, gb_blk, tm=256):
    t, d = h.shape
    return pl.pallas_call(
        _merge_body,
        grid=(t // tm,),
        in_specs=[
            pl.BlockSpec((tm, NA_WIDTH), lambda i: (i, 0)),
            pl.BlockSpec((tm, MLA_WIDTH), lambda i: (i, 0)),
            pl.BlockSpec((tm, d), lambda i: (i, ga_blk)),
            pl.BlockSpec((tm, d), lambda i: (i, gb_blk)),
            pl.BlockSpec((tm, d), lambda i: (i, 0)),
            _resident((NA_WIDTH, d)),
            _resident((MLA_WIDTH, d)),
            _resident((d, d)),
        ],
        out_specs=pl.BlockSpec((tm, d), lambda i: (i, 0)),
        out_shape=jax.ShapeDtypeStruct((t, d), F32),
        compiler_params=_cparams(("parallel",)),
        name="merge",
    )(na_o, mla_o, z, z, h, wa, wb, wo)


def _pl_body(h_ref, p_ref, nw_ref, wg_ref, wp_ref, fn_ref, o_ref, *, final):
    h = h_ref[...]
    xn = _rms(h, nw_ref[...]).astype(BF16)
    gate = _sigmoid(jnp.dot(xn, wg_ref[...], preferred_element_type=F32))
    emb = jnp.dot(p_ref[...].astype(BF16), wp_ref[...], preferred_element_type=F32)
    h = h + gate * emb
    o_ref[...] = _rms(h, fn_ref[...]) if final else h


def _pl_embed(h, p, norm_w, w_gate, w_pl, final_norm, *, final, tm=512):
    t, d = h.shape
    pdim = p.shape[1]
    return pl.pallas_call(
        functools.partial(_pl_body, final=final),
        grid=(t // tm,),
        in_specs=[
            pl.BlockSpec((tm, d), lambda i: (i, 0)),
            pl.BlockSpec((tm, pdim), lambda i: (i, 0)),
            _resident((1, d)),
            _resident((d, d)),
            _resident((pdim, d)),
            _resident((1, d)),
        ],
        out_specs=pl.BlockSpec((tm, d), lambda i: (i, 0)),
        out_shape=jax.ShapeDtypeStruct((t, d), F32),
        compiler_params=_cparams(("parallel",)),
        name="pl_final" if final else "pl_embed",
    )(h, p, norm_w, w_gate, w_pl, final_norm)


def _final_norm_body(h_ref, fn_ref, o_ref):
    o_ref[...] = _rms(h_ref[...], fn_ref[...])


def _rope_tables(seq):
    half = MLA_ROPE_DIM // 2
    pos = jnp.arange(seq, dtype=F32)
    inv_freq = 1.0 / (ROPE_THETA ** (jnp.arange(0, MLA_ROPE_DIM, 2, dtype=F32) / MLA_ROPE_DIM))
    ang = pos[:, None] * inv_freq[None, :]
    cos, sin = jnp.cos(ang), jnp.sin(ang)
    zeros = jnp.zeros((seq, LANES - 2 * half), F32)
    return (jnp.concatenate([cos, cos, zeros], axis=-1),
            jnp.concatenate([-sin, sin, zeros], axis=-1))


def kernel(x, p, ffn1_norm, ffn1_w_gate, ffn1_w_up, ffn1_w_down, mix_norm, w_in, q_a_norm, w_uq, kv_a_norm, w_ukv, na_rpb, w_branch_a, w_branch_b, w_out, ffn2_norm, ffn2_w_gate, ffn2_w_up, ffn2_w_down, pl_norm, w_pl, w_pl_gate, final_norm):
    bsz, seq, d = x.shape
    depth = p.shape[0]
    t = bsz * seq
    bf = lambda w: w.astype(BF16)
    row = lambda g: g.reshape(1, -1).astype(F32)

    c_lat = 3 * NA_WIDTH
    c_rope = c_lat + MLA_Q_RANK + MLA_KV_RANK
    cos_t, sin_t = _rope_tables(seq)

    h = x.reshape(t, d)
    for i in range(depth):
        h, (w_main_t, w_kr_t) = _ffn(
            h, row(ffn1_norm[i]), ffn1_w_gate[i], ffn1_w_up[i], ffn1_w_down[i],
            split_src=jnp.swapaxes(w_in[i], 0, 1), split=(c_rope, MLA_ROPE_DIM))
        z, kr, _ = _in_proj(h, row(mix_norm[i]), w_main_t, w_kr_t)

        wuq = w_uq[i].reshape(MLA_Q_RANK, MLA_HEADS, MLA_QK_DIM)
        wuq_p = bf(jnp.pad(wuq, ((0, 0), (0, 0), (0, MLA_HEAD_PAD - MLA_QK_DIM)))
                   .reshape(MLA_Q_RANK, MLA_HEADS * MLA_HEAD_PAD))
        q_m, k_m, v_m = _mla_prep(
            z, kr, row(q_a_norm[i]), row(kv_a_norm[i]), wuq_p, bf(w_ukv[i]), cos_t, sin_t,
            seq=seq, ql_blk=c_lat // MLA_Q_RANK, kvl_blk=(c_lat + MLA_Q_RANK) // MLA_KV_RANK)

        bias = _na_bias(na_rpb[i].reshape(-1).astype(F32))
        z3 = z.reshape(bsz, seq, z.shape[1])
        na_o = _na_attn(z3, bias, q_blk=0, k_blk=1, v_blk=2)
        later = [ffn2_w_gate[i], ffn2_w_up[i], ffn2_w_down[i], w_branch_a[i], w_branch_b[i],
                 w_out[i], w_pl_gate[i]]
        mla_o, (w2_gate, w2_up, w2_down, wa, wb, wo, w_plg) = _mla_attn(
            q_m.reshape(bsz, seq, -1), k_m.reshape(bsz, seq, -1), v_m.reshape(bsz, seq, -1),
            casts=later)

        gate_col = c_rope
        h = _merge(na_o.reshape(t, NA_WIDTH), mla_o.reshape(t, MLA_WIDTH), z, h, wa, wb, wo,
                   ga_blk=gate_col // d, gb_blk=gate_col // d + 1)

        h, _ = _ffn(h, row(ffn2_norm[i]), w2_gate, w2_up, w2_down)

        h = _pl_embed(h, p[i].reshape(t, -1), row(pl_norm[i]), w_plg, bf(w_pl[i]),
                      row(final_norm), final=(i == depth - 1))

    if depth == 0:
        tm = 512
        h = pl.pallas_call(
            _final_norm_body,
            grid=(t // tm,),
            in_specs=[pl.BlockSpec((tm, d), lambda i: (i, 0)),
                      pl.BlockSpec((1, d), lambda i: (0, 0))],
            out_specs=pl.BlockSpec((tm, d), lambda i: (i, 0)),
            out_shape=jax.ShapeDtypeStruct((t, d), F32),
            compiler_params=_cparams(("parallel",)),
            name="final_norm",
        )(h, row(final_norm))
    return h.reshape(bsz, seq, d)
```

```python
import functools
import math

import jax
import jax.numpy as jnp
from jax import lax
from jax.experimental import pallas as pl
from jax.experimental.pallas import tpu as pltpu

F32 = jnp.float32
BF16 = jnp.bfloat16

GRID_W = 64
NA_HEADS = 8
NA_HEAD_DIM = 128
NA_WIN_ROWS = 8
NA_WIN_COLS = 16
NA_WIDTH = NA_HEADS * NA_HEAD_DIM
MLA_HEADS = 8
MLA_Q_RANK = 512
MLA_KV_RANK = 512
MLA_NOPE_DIM = 128
MLA_ROPE_DIM = 64
MLA_V_DIM = 128
MLA_QK_DIM = MLA_NOPE_DIM + MLA_ROPE_DIM
MLA_WIDTH = MLA_HEADS * MLA_V_DIM
ROPE_THETA = 10000.0
NORM_EPS = 1e-6
NEG_INF = -1e30
LOG2E = math.log2(math.e)

LANES = 128
BF16_ROWS = 16
MLA_HEAD_PAD = 2 * LANES
VMEM_LIMIT_BYTES = 56 * 1024 * 1024


def _cparams(sem):
    return pltpu.CompilerParams(dimension_semantics=sem, vmem_limit_bytes=VMEM_LIMIT_BYTES)


def _resident(shape):
    return pl.BlockSpec(shape, lambda *_: (0, 0), pipeline_mode=pl.Buffered(1))


def _rms(x, g):
    ms = jnp.mean(x * x, axis=-1, keepdims=True)
    return x * lax.rsqrt(ms + NORM_EPS) * g


def _sigmoid(x):
    return 1.0 / (1.0 + jnp.exp(-x))


def _dot_nt(a, b):
    return lax.dot_general(a, b, (((1,), (1,)), ((), ())), preferred_element_type=F32)


def _chunk_rows(rows, nsteps):
    rc = BF16_ROWS
    while rows % rc or rows // rc > nsteps:
        rc += BF16_ROWS
    return rc


def _side_job_specs(casts, split_src, split, nsteps, lin):
    def spec(rows, cols):
        rc = _chunk_rows(rows, nsteps)
        last = rows // rc - 1
        return pl.BlockSpec((rc, cols), lambda *g: (jnp.minimum(lin(*g), last), 0))

    in_specs, out_specs, out_shape, operands = [], [], [], []
    for w in casts:
        in_specs.append(spec(*w.shape))
        out_specs.append(spec(*w.shape))
        out_shape.append(jax.ShapeDtypeStruct(w.shape, BF16))
        operands.append(w)
    if split_src is not None:
        rows, cols = split_src.shape
        keep, skip = split
        assert keep % skip == 0 and rows % skip == 0 and rows // skip <= nsteps and skip <= LANES
        last, hole = rows // skip - 1, keep // skip

        def packed_idx(*g):
            c = jnp.minimum(lin(*g), last)
            return (jnp.where(c <= hole, jnp.minimum(c, hole - 1), c - 1), 0)

        in_specs.append(pl.BlockSpec((skip, cols), lambda *g: (jnp.minimum(lin(*g), last), 0)))
        out_specs += [pl.BlockSpec((skip, cols), packed_idx),
                      pl.BlockSpec((LANES, cols), lambda *g: (0, 0))]
        out_shape += [jax.ShapeDtypeStruct((rows - skip, cols), BF16),
                      jax.ShapeDtypeStruct((LANES, cols), BF16)]
        operands.append(split_src)
    return in_specs, out_specs, out_shape, operands


def _run_side_jobs(in_refs, out_refs, split, step):
    n_cast = len(in_refs) - (1 if split else 0)
    for src, dst in zip(in_refs[:n_cast], out_refs[:n_cast]):
        dst[...] = src[...].astype(BF16)
    if split:
        keep, skip = split
        src = in_refs[n_cast]
        packed_ref, moved_ref = out_refs[n_cast:]
        hole = keep // skip

        @pl.when(step != hole)
        def _():
            packed_ref[...] = src[...].astype(BF16)

        @pl.when(step == hole)
        def _():
            moved_ref[:skip, :] = src[...].astype(BF16)
            if skip < LANES:
                moved_ref[skip:, :] = jnp.zeros((LANES - skip, src.shape[1]), BF16)


def _ffn_up_body(*refs, n_side, split):
    x_ref, nw_ref, wg_ref, wu_ref = refs[:4]
    side_in = refs[4:4 + n_side]
    a_ref = refs[4 + n_side]
    side_out = refs[5 + n_side:-1]
    xn_ref = refs[-1]
    j = pl.program_id(1)

    @pl.when(j == 0)
    def _():
        xn_ref[...] = _rms(x_ref[...], nw_ref[...]).astype(BF16)

    xn = xn_ref[...]
    g = jnp.dot(xn, wg_ref[...].astype(BF16), preferred_element_type=F32)
    u = jnp.dot(xn, wu_ref[...].astype(BF16), preferred_element_type=F32)
    a_ref[...] = (g * _sigmoid(g) * u * 0.5).astype(BF16)
    _run_side_jobs(side_in, side_out, split, pl.program_id(0) * pl.num_programs(1) + j)


def _ffn_down_body(a_ref, wd_ref, x_ref, o_ref):
    o_ref[...] = x_ref[...] + jnp.dot(a_ref[...], wd_ref[...], preferred_element_type=F32)


def _ffn(x, norm_w, wg, wu, wd, *, casts=(), split_src=None, split=None, tm=1024, tf=512, tn=512):
    t, d = x.shape
    f = wg.shape[1]
    nj = f // tf
    casts = list(casts)
    if wd.dtype != BF16:
        casts.append(wd)
    side = _side_job_specs(casts, split_src, split, (t // tm) * nj, lambda i, j: i * nj + j)
    outs = pl.pallas_call(
        functools.partial(_ffn_up_body, n_side=len(side[0]),
                          split=split if split_src is not None else None),
        grid=(t // tm, nj),
        in_specs=[
            pl.BlockSpec((tm, d), lambda i, j: (i, 0)),
            pl.BlockSpec((1, d), lambda i, j: (0, 0)),
            pl.BlockSpec((d, tf), lambda i, j: (0, j)),
            pl.BlockSpec((d, tf), lambda i, j: (0, j)),
        ] + side[0],
        out_specs=[pl.BlockSpec((tm, tf), lambda i, j: (i, j))] + side[1],
        out_shape=[jax.ShapeDtypeStruct((t, f), BF16)] + side[2],
        scratch_shapes=[pltpu.VMEM((tm, d), BF16)],
        compiler_params=_cparams(("arbitrary", "arbitrary")),
        name="ffn_up",
    )(x, norm_w, wg, wu, *side[3])
    a = outs[0]
    converted = list(outs[1:])
    if wd.dtype != BF16:
        wd = converted.pop(len(casts) - 1)
    h = pl.pallas_call(
        _ffn_down_body,
        grid=(t // tm, d // tn),
        in_specs=[
            pl.BlockSpec((tm, f), lambda i, j: (i, 0)),
            pl.BlockSpec((f, tn), lambda i, j: (0, j)),
            pl.BlockSpec((tm, tn), lambda i, j: (i, j)),
        ],
        out_specs=pl.BlockSpec((tm, tn), lambda i, j: (i, j)),
        out_shape=jax.ShapeDtypeStruct((t, d), F32),
        compiler_params=_cparams(("parallel", "arbitrary")),
        name="ffn_down",
    )(a, wd, x)
    return h, converted


def _in_proj_body(*refs, n_side):
    h_ref, nw_ref, w_ref, wkr_ref = refs[:4]
    side_in = refs[4:4 + n_side]
    z_ref, kr_ref = refs[4 + n_side:6 + n_side]
    side_out = refs[6 + n_side:-1]
    xn_ref = refs[-1]
    j = pl.program_id(1)

    @pl.when(j == 0)
    def _():
        xn = _rms(h_ref[...], nw_ref[...]).astype(BF16)
        xn_ref[...] = xn
        kr_ref[...] = _dot_nt(xn, wkr_ref[...])

    z_ref[...] = _dot_nt(xn_ref[...], w_ref[...]).astype(BF16)
    _run_side_jobs(side_in, side_out, None, None)


def _in_proj(h, norm_w, w_main_t, w_kr_t, *, casts=(), tm=1024, tn=2048):
    t, d = h.shape
    n = w_main_t.shape[0]
    nj = n // tn
    side = _side_job_specs(casts, None, None, (t // tm) * nj, lambda i, j: i * nj + j)
    outs = pl.pallas_call(
        functools.partial(_in_proj_body, n_side=len(side[0])),
        grid=(t // tm, nj),
        in_specs=[
            pl.BlockSpec((tm, d), lambda i, j: (i, 0)),
            pl.BlockSpec((1, d), lambda i, j: (0, 0)),
            pl.BlockSpec((tn, d), lambda i, j: (j, 0)),
            pl.BlockSpec((LANES, d), lambda i, j: (0, 0)),
        ] + side[0],
        out_specs=[
            pl.BlockSpec((tm, tn), lambda i, j: (i, j)),
            pl.BlockSpec((tm, LANES), lambda i, j: (i, 0)),
        ] + side[1],
        out_shape=[
            jax.ShapeDtypeStruct((t, n), BF16),
            jax.ShapeDtypeStruct((t, LANES), F32),
        ] + side[2],
        scratch_shapes=[pltpu.VMEM((tm, d), BF16)],
        compiler_params=_cparams(("arbitrary", "arbitrary")),
        name="in_proj",
    )(h, norm_w, w_main_t, w_kr_t, *side[3])
    return outs[0], outs[1], list(outs[2:])


def _rope(r, cos_t, sin_t):
    lane = lax.broadcasted_iota(jnp.int32, r.shape, 1)
    half = MLA_ROPE_DIM // 2
    swapped = jnp.where(lane < half, pltpu.roll(r, LANES - half, 1), pltpu.roll(r, half, 1))
    return r * cos_t + swapped * sin_t


def _mla_prep_body(ql_ref, kvl_ref, kr_ref, qn_ref, kvn_ref, wuq_ref, wukv_ref,
                   cos_ref, sin_ref, q_out, k_out, v_out):
    cq = _rms(ql_ref[...].astype(F32), qn_ref[...]).astype(BF16)
    ckv = _rms(kvl_ref[...].astype(F32), kvn_ref[...]).astype(BF16)
    cos_t = cos_ref[...]
    sin_t = sin_ref[...]
    k_rope = _rope(kr_ref[...], cos_t, sin_t).astype(BF16)
    for h in range(MLA_HEADS):
        lo = h * MLA_HEAD_PAD
        mid = lo + LANES
        hi = lo + MLA_HEAD_PAD
        q = jnp.dot(cq, wuq_ref[:, lo:hi], preferred_element_type=F32)
        kv = jnp.dot(ckv, wukv_ref[:, lo:hi], preferred_element_type=F32)
        q_out[:, lo:mid] = q[:, :LANES].astype(BF16)
        q_out[:, mid:hi] = _rope(q[:, LANES:], cos_t, sin_t).astype(BF16)
        k_out[:, lo:mid] = kv[:, :LANES].astype(BF16)
        k_out[:, mid:hi] = k_rope
        v_out[:, lo:mid] = kv[:, LANES:].astype(BF16)
        v_out[:, mid:hi] = jnp.ones((kv.shape[0], MLA_HEAD_PAD - MLA_V_DIM), BF16)


def _mla_prep(z, kr, q_norm, kv_norm, wuq_p, wukv, cos_t, sin_t, *, seq, ql_blk, kvl_blk, tm=512):
    t = z.shape[0]
    nseq = seq // tm
    qw = MLA_HEADS * MLA_HEAD_PAD
    return pl.pallas_call(
        _mla_prep_body,
        grid=(t // tm,),
        in_specs=[
            pl.BlockSpec((tm, MLA_Q_RANK), lambda i: (i, ql_blk)),
            pl.BlockSpec((tm, MLA_KV_RANK), lambda i: (i, kvl_blk)),
            pl.BlockSpec((tm, LANES), lambda i: (i, 0)),
            pl.BlockSpec((1, MLA_Q_RANK), lambda i: (0, 0)),
            pl.BlockSpec((1, MLA_KV_RANK), lambda i: (0, 0)),
            pl.BlockSpec((MLA_Q_RANK, qw), lambda i: (0, 0)),
            pl.BlockSpec((MLA_KV_RANK, qw), lambda i: (0, 0)),
            pl.BlockSpec((tm, LANES), lambda i: (i % nseq, 0)),
            pl.BlockSpec((tm, LANES), lambda i: (i % nseq, 0)),
        ],
        out_specs=[
            pl.BlockSpec((tm, qw), lambda i: (i, 0)),
            pl.BlockSpec((tm, qw), lambda i: (i, 0)),
            pl.BlockSpec((tm, qw), lambda i: (i, 0)),
        ],
        out_shape=[
            jax.ShapeDtypeStruct((t, qw), BF16),
            jax.ShapeDtypeStruct((t, qw), BF16),
            jax.ShapeDtypeStruct((t, qw), BF16),
        ],
        compiler_params=_cparams(("parallel",)),
        name="mla_prep",
    )(z, z, kr, q_norm, kv_norm, wuq_p, wukv, cos_t, sin_t)


RPB_ROWS = 2 * NA_WIN_ROWS - 1
RPB_COLS = 2 * NA_WIN_COLS - 1
NA_Q_ROWS = 4
NA_K_ROWS = NA_Q_ROWS + NA_WIN_ROWS
NA_VARIANTS = 3


def _na_variant_geometry(var, j):
    lo = (0, j, NA_K_ROWS - NA_WIN_ROWS)[var]
    rho0 = (NA_WIN_ROWS - 1) - j - (NA_WIN_ROWS // 2) * var
    return lo, rho0


def _na_bias_body(rpb_ref, o_ref, pair_ref):
    h = pl.program_id(0)
    kw = NA_WIN_COLS
    shape = (GRID_W, LANES)
    q = lax.broadcasted_iota(jnp.int32, shape, 0)
    lane = lax.broadcasted_iota(jnp.int32, shape, 1)
    first = lane < GRID_W
    k = jnp.where(first, lane, lane - GRID_W)
    col_start = jnp.clip(q - kw // 2, 0, GRID_W - kw)
    valid = (k >= col_start) & (k < col_start + kw)
    dc = jnp.clip(k - q, -(kw - 1), kw - 1) + (kw - 1)
    neg = jnp.full(shape, NEG_INF, F32)

    for e in range(RPB_ROWS + 1):
        base_a = (h * RPB_ROWS + max(e - 1, 0)) * RPB_COLS
        base_b = (h * RPB_ROWS + min(e, RPB_ROWS - 1)) * RPB_COLS
        acc = jnp.zeros(shape, F32)
        for c in range(RPB_COLS):
            val = jnp.where(first, rpb_ref[base_a + c], rpb_ref[base_b + c])
            acc = jnp.where(dc == c, val, acc)
        pair_ref[e] = jnp.where(valid, acc * LOG2E, NEG_INF)

    for var in range(NA_VARIANTS):
        for j in range(NA_Q_ROWS):
            lo, rho0 = _na_variant_geometry(var, j)
            for pair in range(NA_K_ROWS // 2):
                i = 2 * pair
                in_a = lo <= i < lo + NA_WIN_ROWS
                in_b = lo <= i + 1 < lo + NA_WIN_ROWS
                if in_a or in_b:
                    tile = pair_ref[rho0 + i + 1]
                    if not in_a:
                        tile = jnp.where(first, neg, tile)
                    if not in_b:
                        tile = jnp.where(first, tile, neg)
                else:
                    tile = neg
                o_ref[var, 0, j * GRID_W:(j + 1) * GRID_W, pair * LANES:(pair + 1) * LANES] = tile


def _na_bias(rpb_flat):
    nq = NA_Q_ROWS * GRID_W
    nk = NA_K_ROWS * GRID_W
    return pl.pallas_call(
        _na_bias_body,
        grid=(NA_HEADS,),
        in_specs=[pl.BlockSpec(memory_space=pltpu.SMEM)],
        out_specs=pl.BlockSpec((NA_VARIANTS, 1, nq, nk), lambda h: (0, h, 0, 0)),
        out_shape=jax.ShapeDtypeStruct((NA_VARIANTS, NA_HEADS, nq, nk), F32),
        scratch_shapes=[pltpu.VMEM((RPB_ROWS + 1, GRID_W, LANES), F32)],
        compiler_params=_cparams(("arbitrary",)),
        name="na_bias",
    )(rpb_flat)


def _na_key_start(g, rows):
    return jnp.clip(g * NA_Q_ROWS - NA_WIN_ROWS // 2, 0, rows - NA_K_ROWS)


def _na_body(q_ref, k_ref, v_ref, b_ref, o_ref, *, rows):
    g = pl.program_id(1)
    start = pl.multiple_of(_na_key_start(g, rows) * GRID_W, GRID_W)
    kind = jnp.where(g == 0, 0, jnp.where(g == rows // NA_Q_ROWS - 1, 2, 1))
    nkeys = NA_K_ROWS * GRID_W
    c = (NA_HEAD_DIM ** -0.5) * LOG2E
    ones = jnp.ones((nkeys, NA_HEAD_DIM), BF16)
    for h in range(NA_HEADS):
        cs = slice(h * NA_HEAD_DIM, (h + 1) * NA_HEAD_DIM)
        q = q_ref[0, :, cs]
        k = k_ref[0, pl.ds(start, nkeys), cs]
        v = v_ref[0, pl.ds(start, nkeys), cs]
        s = _dot_nt(q, k)
        t = s * c + b_ref[kind, h]
        m = jnp.max(t, axis=-1, keepdims=True)
        p = jnp.exp2(t - m).astype(BF16)
        oe = jnp.dot(p, jnp.concatenate([v, ones], axis=1), preferred_element_type=F32)
        o_ref[0, :, cs] = (oe[:, :NA_HEAD_DIM] / oe[:, NA_HEAD_DIM:]).astype(BF16)


def _na_attn(z3, bias, *, q_blk, k_blk, v_blk):
    b, s, _ = z3.shape
    rows = s // GRID_W
    assert rows % NA_Q_ROWS == 0 and rows >= 2 * NA_K_ROWS - NA_WIN_ROWS
    groups = rows // NA_Q_ROWS
    nq = NA_Q_ROWS * GRID_W
    return pl.pallas_call(
        functools.partial(_na_body, rows=rows),
        grid=(b, groups),
        in_specs=[
            pl.BlockSpec((1, nq, NA_WIDTH), lambda bi, g: (bi, g, q_blk)),
            pl.BlockSpec((1, s, NA_WIDTH), lambda bi, g: (bi, 0, k_blk)),
            pl.BlockSpec((1, s, NA_WIDTH), lambda bi, g: (bi, 0, v_blk)),
            pl.BlockSpec(bias.shape, lambda bi, g: (0, 0, 0, 0), pipeline_mode=pl.Buffered(1)),
        ],
        out_specs=pl.BlockSpec((1, nq, NA_WIDTH), lambda bi, g: (bi, g, 0)),
        out_shape=jax.ShapeDtypeStruct((b, s, NA_WIDTH), BF16),
        compiler_params=_cparams(("parallel", "arbitrary")),
        name="na_attn",
    )(z3, z3, z3, bias)


def _mla_body(*refs, n_side, tq, tk):
    q_ref, k_ref, v_ref = refs[:3]
    o_ref = refs[3 + n_side]
    _run_side_jobs(refs[3:3 + n_side], refs[4 + n_side:], None, None)
    seq = q_ref.shape[1]
    c = (MLA_QK_DIM ** -0.5) * LOG2E
    for qi in range(seq // tq):
        q = q_ref[0, qi * tq:(qi + 1) * tq, :]
        m = acc = None
        for kj in range(seq // tk):
            ks = slice(kj * tk, (kj + 1) * tk)
            s = _dot_nt(q, k_ref[0, ks, :]) * c
            mj = jnp.max(s, axis=-1, keepdims=True)
            m_new = mj if m is None else jnp.maximum(m, mj)
            p = jnp.exp2(s - m_new).astype(BF16)
            pv = jnp.dot(p, v_ref[0, ks, :], preferred_element_type=F32)
            acc = pv if acc is None else acc * jnp.exp2(m - m_new) + pv
            m = m_new
        o_ref[0, qi * tq:(qi + 1) * tq, :] = (acc[:, :MLA_V_DIM] / acc[:, MLA_V_DIM:]).astype(BF16)


def _mla_attn(q3, k3, v3, *, casts=(), tq=512, tk=512):
    b, s, _ = q3.shape
    vw = v3.shape[2] // MLA_HEADS
    side = _side_job_specs(casts, None, None, b * MLA_HEADS, lambda bi, h: bi * MLA_HEADS + h)
    outs = pl.pallas_call(
        functools.partial(_mla_body, n_side=len(side[0]), tq=tq, tk=tk),
        grid=(b, MLA_HEADS),
        in_specs=[
            pl.BlockSpec((1, s, MLA_HEAD_PAD), lambda bi, h: (bi, 0, h)),
            pl.BlockSpec((1, s, MLA_HEAD_PAD), lambda bi, h: (bi, 0, h)),
            pl.BlockSpec((1, s, vw), lambda bi, h: (bi, 0, h)),
        ] + side[0],
        out_specs=[pl.BlockSpec((1, s, MLA_V_DIM), lambda bi, h: (bi, 0, h))] + side[1],
        out_shape=[jax.ShapeDtypeStruct((b, s, MLA_WIDTH), BF16)] + side[2],
        compiler_params=_cparams(("arbitrary", "arbitrary")),
        name="mla_attn",
    )(q3, k3, v3, *side[3])
    return outs[0], list(outs[1:])


def _merge_body(na_ref, mla_ref, ga_ref, gb_ref, h_ref, wa_ref, wb_ref, wo_ref, o_ref):
    ya = jnp.dot(na_ref[...], wa_ref[...], preferred_element_type=F32)
    yb = jnp.dot(mla_ref[...], wb_ref[...], preferred_element_type=F32)
    merged = _sigmoid(ga_ref[...].astype(F32)) * ya + _sigmoid(gb_ref[...].astype(F32)) * yb
    o_ref[...] = h_ref[...] + jnp.dot(merged.astype(BF16), wo_ref[...],
                                      preferred_element_type=F32)


def _merge(na_o, mla_o, z, h, wa, wb, wo, *, ga_blk, gb_blk, tm=256):
    t, d = h.shape
    return pl.pallas_call(
        _merge_body,
        grid=(t // tm,),
        in_specs=[
            pl.BlockSpec((tm, NA_WIDTH), lambda i: (i, 0)),
            pl.BlockSpec((tm, MLA_WIDTH), lambda i: (i, 0)),
            pl.BlockSpec((tm, d), lambda i: (i, ga_blk)),
            pl.BlockSpec((tm, d), lambda i: (i, gb_blk)),
            pl.BlockSpec((tm, d), lambda i: (i, 0)),
            _resident((NA_WIDTH, d)),
            _resident((MLA_WIDTH, d)),
            _resident((d, d)),
        ],
        out_specs=pl.BlockSpec((tm, d), lambda i: (i, 0)),
        out_shape=jax.ShapeDtypeStruct((t, d), F32),
        compiler_params=_cparams(("parallel",)),
        name="merge",
    )(na_o, mla_o, z, z, h, wa, wb, wo)


def _pl_body(h_ref, p_ref, nw_ref, wg_ref, wp_ref, fn_ref, o_ref, *, final):
    h = h_ref[...]
    xn = _rms(h, nw_ref[...]).astype(BF16)
    gate = _sigmoid(jnp.dot(xn, wg_ref[...], preferred_element_type=F32))
    emb = jnp.dot(p_ref[...].astype(BF16), wp_ref[...], preferred_element_type=F32)
    h = h + gate * emb
    o_ref[...] = _rms(h, fn_ref[...]) if final else h


def _pl_embed(h, p, norm_w, w_gate, w_pl, final_norm, *, final, tm=512):
    t, d = h.shape
    pdim = p.shape[1]
    return pl.pallas_call(
        functools.partial(_pl_body, final=final),
        grid=(t // tm,),
        in_specs=[
            pl.BlockSpec((tm, d), lambda i: (i, 0)),
            pl.BlockSpec((tm, pdim), lambda i: (i, 0)),
            _resident((1, d)),
            _resident((d, d)),
            _resident((pdim, d)),
            _resident((1, d)),
        ],
        out_specs=pl.BlockSpec((tm, d), lambda i: (i, 0)),
        out_shape=jax.ShapeDtypeStruct((t, d), F32),
        compiler_params=_cparams(("parallel",)),
        name="pl_final" if final else "pl_embed",
    )(h, p, norm_w, w_gate, w_pl, final_norm)


def _final_norm_body(h_ref, fn_ref, o_ref):
    o_ref[...] = _rms(h_ref[...], fn_ref[...])


def _rope_tables(seq):
    half = MLA_ROPE_DIM // 2
    pos = jnp.arange(seq, dtype=F32)
    inv_freq = 1.0 / (ROPE_THETA ** (jnp.arange(0, MLA_ROPE_DIM, 2, dtype=F32) / MLA_ROPE_DIM))
    ang = pos[:, None] * inv_freq[None, :]
    cos, sin = jnp.cos(ang), jnp.sin(ang)
    zeros = jnp.zeros((seq, LANES - 2 * half), F32)
    return (jnp.concatenate([cos, cos, zeros], axis=-1),
            jnp.concatenate([-sin, sin, zeros], axis=-1))


def kernel(x, p, ffn1_norm, ffn1_w_gate, ffn1_w_up, ffn1_w_down, mix_norm, w_in, q_a_norm, w_uq, kv_a_norm, w_ukv, na_rpb, w_branch_a, w_branch_b, w_out, ffn2_norm, ffn2_w_gate, ffn2_w_up, ffn2_w_down, pl_norm, w_pl, w_pl_gate, final_norm):
    bsz, seq, d = x.shape
    depth = p.shape[0]
    t = bsz * seq
    bf = lambda w: w.astype(BF16)
    row = lambda g: g.reshape(1, -1).astype(F32)

    c_lat = 3 * NA_WIDTH
    c_rope = c_lat + MLA_Q_RANK + MLA_KV_RANK
    cos_t, sin_t = _rope_tables(seq)

    h = x.reshape(t, d)
    for i in range(depth):
        h, (w_main_t, w_kr_t) = _ffn(
            h, row(ffn1_norm[i]), ffn1_w_gate[i], ffn1_w_up[i], ffn1_w_down[i],
            split_src=jnp.swapaxes(w_in[i], 0, 1), split=(c_rope, MLA_ROPE_DIM))
        z, kr, _ = _in_proj(h, row(mix_norm[i]), w_main_t, w_kr_t)

        wuq = w_uq[i].reshape(MLA_Q_RANK, MLA_HEADS, MLA_QK_DIM)
        wuq_p = bf(jnp.pad(wuq, ((0, 0), (0, 0), (0, MLA_HEAD_PAD - MLA_QK_DIM)))
                   .reshape(MLA_Q_RANK, MLA_HEADS * MLA_HEAD_PAD))
        q_m, k_m, v_m = _mla_prep(
            z, kr, row(q_a_norm[i]), row(kv_a_norm[i]), wuq_p, bf(w_ukv[i]), cos_t, sin_t,
            seq=seq, ql_blk=c_lat // MLA_Q_RANK, kvl_blk=(c_lat + MLA_Q_RANK) // MLA_KV_RANK)

        bias = _na_bias(na_rpb[i].reshape(-1).astype(F32))
        z3 = z.reshape(bsz, seq, z.shape[1])
        na_o = _na_attn(z3, bias, q_blk=0, k_blk=1, v_blk=2)
        later = [ffn2_w_gate[i], ffn2_w_up[i], ffn2_w_down[i], w_branch_a[i], w_branch_b[i],
                 w_out[i], w_pl_gate[i]]
        mla_o, (w2_gate, w2_up, w2_down, wa, wb, wo, w_plg) = _mla_attn(
            q_m.reshape(bsz, seq, -1), k_m.reshape(bsz, seq, -1), v_m.reshape(bsz, seq, -1),
            casts=later)

        gate_col = c_rope
        h = _merge(na_o.reshape(t, NA_WIDTH), mla_o.reshape(t, MLA_WIDTH), z, h, wa, wb, wo,
                   ga_blk=gate_col // d, gb_blk=gate_col // d + 1)

        h, _ = _ffn(h, row(ffn2_norm[i]), w2_gate, w2_up, w2_down)

        h = _pl_embed(h, p[i].reshape(t, -1), row(pl_norm[i]), w_plg, bf(w_pl[i]),
                      row(final_norm), final=(i == depth - 1))

    if depth == 0:
        tm = 512
        h = pl.pallas_call(
            _final_norm_body,
            grid=(t // tm,),
            in_specs=[pl.BlockSpec((tm, d), lambda i: (i, 0)),
                      pl.BlockSpec((1, d), lambda i: (0, 0))],
            out_specs=pl.BlockSpec((tm, d), lambda i: (i, 0)),
            out_shape=jax.ShapeDtypeStruct((t, d), F32),
            compiler_params=_cparams(("parallel",)),
            name="final_norm",
        )(h, row(final_norm))
    return h.reshape(bsz, seq, d)
```

```python
import functools
import math

import jax
import jax.numpy as jnp
from jax import lax
from jax.experimental import pallas as pl
from jax.experimental.pallas import tpu as pltpu

F32 = jnp.float32
BF16 = jnp.bfloat16

GRID_W = 64
NA_HEADS = 8
NA_HEAD_DIM = 128
NA_WIN_ROWS = 8
NA_WIN_COLS = 16
NA_WIDTH = NA_HEADS * NA_HEAD_DIM
MLA_HEADS = 8
MLA_Q_RANK = 512
MLA_KV_RANK = 512
MLA_NOPE_DIM = 128
MLA_ROPE_DIM = 64
MLA_V_DIM = 128
MLA_QK_DIM = MLA_NOPE_DIM + MLA_ROPE_DIM
MLA_WIDTH = MLA_HEADS * MLA_V_DIM
ROPE_THETA = 10000.0
NORM_EPS = 1e-6
NEG_INF = -1e30
LOG2E = math.log2(math.e)

LANES = 128
BF16_ROWS = 16
MLA_HEAD_PAD = 2 * LANES
VMEM_LIMIT_BYTES = 56 * 1024 * 1024


def _cparams(sem):
    return pltpu.CompilerParams(dimension_semantics=sem, vmem_limit_bytes=VMEM_LIMIT_BYTES)


def _resident(shape):
    return pl.BlockSpec(shape, lambda *_: (0, 0), pipeline_mode=pl.Buffered(1))


def _rms(x, g):
    ms = jnp.mean(x * x, axis=-1, keepdims=True)
    return x * lax.rsqrt(ms + NORM_EPS) * g


def _sigmoid(x):
    return 1.0 / (1.0 + jnp.exp(-x))


def _dot_nt(a, b):
    return lax.dot_general(a, b, (((1,), (1,)), ((), ())), preferred_element_type=F32)


def _chunk_rows(rows, nsteps):
    rc = BF16_ROWS
    while rows % rc or rows // rc > nsteps:
        rc += BF16_ROWS
    return rc


def _side_job_specs(casts, split_src, split, nsteps, lin):
    def spec(rows, cols):
        rc = _chunk_rows(rows, nsteps)
        last = rows // rc - 1
        return pl.BlockSpec((rc, cols), lambda *g: (jnp.minimum(lin(*g), last), 0))

    in_specs, out_specs, out_shape, operands = [], [], [], []
    for w in casts:
        in_specs.append(spec(*w.shape))
        out_specs.append(spec(*w.shape))
        out_shape.append(jax.ShapeDtypeStruct(w.shape, BF16))
        operands.append(w)
    if split_src is not None:
        rows, cols = split_src.shape
        keep, skip = split
        assert keep % skip == 0 and rows % skip == 0 and rows // skip <= nsteps and skip <= LANES
        last, hole = rows // skip - 1, keep // skip

        def packed_idx(*g):
            c = jnp.minimum(lin(*g), last)
            return (jnp.where(c <= hole, jnp.minimum(c, hole - 1), c - 1), 0)

        in_specs.append(pl.BlockSpec((skip, cols), lambda *g: (jnp.minimum(lin(*g), last), 0)))
        out_specs += [pl.BlockSpec((skip, cols), packed_idx),
                      pl.BlockSpec((LANES, cols), lambda *g: (0, 0))]
        out_shape += [jax.ShapeDtypeStruct((rows - skip, cols), BF16),
                      jax.ShapeDtypeStruct((LANES, cols), BF16)]
        operands.append(split_src)
    return in_specs, out_specs, out_shape, operands


def _run_side_jobs(in_refs, out_refs, split, step):
    n_cast = len(in_refs) - (1 if split else 0)
    for src, dst in zip(in_refs[:n_cast], out_refs[:n_cast]):
        dst[...] = src[...].astype(BF16)
    if split:
        keep, skip = split
        src = in_refs[n_cast]
        packed_ref, moved_ref = out_refs[n_cast:]
        hole = keep // skip

        @pl.when(step != hole)
        def _():
            packed_ref[...] = src[...].astype(BF16)

        @pl.when(step == hole)
        def _():
            moved_ref[:skip, :] = src[...].astype(BF16)
            if skip < LANES:
                moved_ref[skip:, :] = jnp.zeros((LANES - skip, src.shape[1]), BF16)


FFN_UP_ROWS = 1024


def _ffn_up_body(*refs, n_side, split, prenormed):
    x_ref, nw_ref, wg_ref, wu_ref = refs[:4]
    side_in = refs[4:4 + n_side]
    a_ref = refs[4 + n_side]
    j = pl.program_id(1)
    if prenormed:
        side_out = refs[5 + n_side:]
        xn_ref = x_ref
    else:
        side_out = refs[5 + n_side:-1]
        xn_ref = refs[-1]

        @pl.when(j == 0)
        def _():
            xn_ref[...] = _rms(x_ref[...], nw_ref[...]).astype(BF16)

    wg = wg_ref[...].astype(BF16)
    wu = wu_ref[...].astype(BF16)
    rows_per = min(xn_ref.shape[0], FFN_UP_ROWS)
    for r in range(xn_ref.shape[0] // rows_per):
        rows = slice(r * rows_per, (r + 1) * rows_per)
        xn = xn_ref[rows, :]
        g = jnp.dot(xn, wg, preferred_element_type=F32)
        u = jnp.dot(xn, wu, preferred_element_type=F32)
        a_ref[rows, :] = (g * _sigmoid(g) * u * 0.5).astype(BF16)
    _run_side_jobs(side_in, side_out, split, pl.program_id(0) * pl.num_programs(1) + j)


def _ffn_down_body(a_ref, wd_ref, x_ref, o_ref):
    o_ref[...] = x_ref[...] + jnp.dot(a_ref[...], wd_ref[...], preferred_element_type=F32)


def _ffn(x, norm_w, wg, wu, wd, *, xn=None, casts=(), split_src=None, split=None,
         tm=1024, tf=512, tn=512):
    t, d = x.shape
    f = wg.shape[1]
    nj = f // tf
    casts = list(casts)
    if wd.dtype != BF16:
        casts.append(wd)
    prenormed = xn is not None
    tm_up = 2 * tm if prenormed else tm
    side = _side_job_specs(casts, split_src, split, (t // tm_up) * nj, lambda i, j: i * nj + j)
    outs = pl.pallas_call(
        functools.partial(_ffn_up_body, n_side=len(side[0]), prenormed=prenormed,
                          split=split if split_src is not None else None),
        grid=(t // tm_up, nj),
        in_specs=[
            pl.BlockSpec((tm_up, d), lambda i, j: (i, 0)),
            pl.BlockSpec((1, d), lambda i, j: (0, 0)),
            pl.BlockSpec((d, tf), lambda i, j: (0, j)),
            pl.BlockSpec((d, tf), lambda i, j: (0, j)),
        ] + side[0],
        out_specs=[pl.BlockSpec((tm_up, tf), lambda i, j: (i, j))] + side[1],
        out_shape=[jax.ShapeDtypeStruct((t, f), BF16)] + side[2],
        scratch_shapes=[] if prenormed else [pltpu.VMEM((tm, d), BF16)],
        compiler_params=_cparams(("arbitrary", "arbitrary")),
        name="ffn_up",
    )(xn if prenormed else x, norm_w, wg, wu, *side[3])
    a = outs[0]
    converted = list(outs[1:])
    if wd.dtype != BF16:
        wd = converted.pop(len(casts) - 1)
    h = pl.pallas_call(
        _ffn_down_body,
        grid=(t // tm, d // tn),
        in_specs=[
            pl.BlockSpec((tm, f), lambda i, j: (i, 0)),
            pl.BlockSpec((f, tn), lambda i, j: (0, j)),
            pl.BlockSpec((tm, tn), lambda i, j: (i, j)),
        ],
        out_specs=pl.BlockSpec((tm, tn), lambda i, j: (i, j)),
        out_shape=jax.ShapeDtypeStruct((t, d), F32),
        compiler_params=_cparams(("parallel", "arbitrary")),
        name="ffn_down",
    )(a, wd, x)
    return h, converted


def _in_proj_body(*refs, n_side):
    h_ref, nw_ref, w_ref, wkr_ref = refs[:4]
    side_in = refs[4:4 + n_side]
    z_ref, kr_ref = refs[4 + n_side:6 + n_side]
    side_out = refs[6 + n_side:-1]
    xn_ref = refs[-1]
    j = pl.program_id(1)

    @pl.when(j == 0)
    def _():
        xn = _rms(h_ref[...], nw_ref[...]).astype(BF16)
        xn_ref[...] = xn
        kr_ref[...] = _dot_nt(xn, wkr_ref[...])

    z_ref[...] = _dot_nt(xn_ref[...], w_ref[...]).astype(BF16)
    _run_side_jobs(side_in, side_out, None, None)


def _in_proj(h, norm_w, w_main_t, w_kr_t, *, casts=(), tm=1024, tn=2048):
    t, d = h.shape
    n = w_main_t.shape[0]
    nj = n // tn
    side = _side_job_specs(casts, None, None, (t // tm) * nj, lambda i, j: i * nj + j)
    outs = pl.pallas_call(
        functools.partial(_in_proj_body, n_side=len(side[0])),
        grid=(t // tm, nj),
        in_specs=[
            pl.BlockSpec((tm, d), lambda i, j: (i, 0)),
            pl.BlockSpec((1, d), lambda i, j: (0, 0)),
            pl.BlockSpec((tn, d), lambda i, j: (j, 0)),
            pl.BlockSpec((LANES, d), lambda i, j: (0, 0)),
        ] + side[0],
        out_specs=[
            pl.BlockSpec((tm, tn), lambda i, j: (i, j)),
            pl.BlockSpec((tm, LANES), lambda i, j: (i, 0)),
        ] + side[1],
        out_shape=[
            jax.ShapeDtypeStruct((t, n), BF16),
            jax.ShapeDtypeStruct((t, LANES), F32),
        ] + side[2],
        scratch_shapes=[pltpu.VMEM((tm, d), BF16)],
        compiler_params=_cparams(("arbitrary", "arbitrary")),
        name="in_proj",
    )(h, norm_w, w_main_t, w_kr_t, *side[3])
    return outs[0], outs[1], list(outs[2:])


def _rope(r, cos_t, sin_t):
    lane = lax.broadcasted_iota(jnp.int32, r.shape, 1)
    half = MLA_ROPE_DIM // 2
    swapped = jnp.where(lane < half, pltpu.roll(r, LANES - half, 1), pltpu.roll(r, half, 1))
    return r * cos_t + swapped * sin_t


def _mla_prep_body(ql_ref, kvl_ref, kr_ref, qn_ref, kvn_ref, wuq_ref, wukv_ref,
                   cos_ref, sin_ref, q_out, k_out, v_out):
    cq = _rms(ql_ref[...].astype(F32), qn_ref[...]).astype(BF16)
    ckv = _rms(kvl_ref[...].astype(F32), kvn_ref[...]).astype(BF16)
    cos_t = cos_ref[...]
    sin_t = sin_ref[...]
    k_rope = _rope(kr_ref[...], cos_t, sin_t).astype(BF16)
    for h in range(MLA_HEADS):
        lo = h * MLA_HEAD_PAD
        mid = lo + LANES
        hi = lo + MLA_HEAD_PAD
        q = jnp.dot(cq, wuq_ref[:, lo:hi], preferred_element_type=F32)
        kv = jnp.dot(ckv, wukv_ref[:, lo:hi], preferred_element_type=F32)
        q_out[:, lo:mid] = q[:, :LANES].astype(BF16)
        q_out[:, mid:hi] = _rope(q[:, LANES:], cos_t, sin_t).astype(BF16)
        k_out[:, lo:mid] = kv[:, :LANES].astype(BF16)
        k_out[:, mid:hi] = k_rope
        v_out[:, lo:mid] = kv[:, LANES:].astype(BF16)
        v_out[:, mid:hi] = jnp.ones((kv.shape[0], MLA_HEAD_PAD - MLA_V_DIM), BF16)


def _mla_prep(z, kr, q_norm, kv_norm, wuq_p, wukv, cos_t, sin_t, *, seq, ql_blk, kvl_blk, tm=512):
    t = z.shape[0]
    nseq = seq // tm
    qw = MLA_HEADS * MLA_HEAD_PAD
    return pl.pallas_call(
        _mla_prep_body,
        grid=(t // tm,),
        in_specs=[
            pl.BlockSpec((tm, MLA_Q_RANK), lambda i: (i, ql_blk)),
            pl.BlockSpec((tm, MLA_KV_RANK), lambda i: (i, kvl_blk)),
            pl.BlockSpec((tm, LANES), lambda i: (i, 0)),
            pl.BlockSpec((1, MLA_Q_RANK), lambda i: (0, 0)),
            pl.BlockSpec((1, MLA_KV_RANK), lambda i: (0, 0)),
            pl.BlockSpec((MLA_Q_RANK, qw), lambda i: (0, 0)),
            pl.BlockSpec((MLA_KV_RANK, qw), lambda i: (0, 0)),
            pl.BlockSpec((tm, LANES), lambda i: (i % nseq, 0)),
            pl.BlockSpec((tm, LANES), lambda i: (i % nseq, 0)),
        ],
        out_specs=[
            pl.BlockSpec((tm, qw), lambda i: (i, 0)),
            pl.BlockSpec((tm, qw), lambda i: (i, 0)),
            pl.BlockSpec((tm, qw), lambda i: (i, 0)),
        ],
        out_shape=[
            jax.ShapeDtypeStruct((t, qw), BF16),
            jax.ShapeDtypeStruct((t, qw), BF16),
            jax.ShapeDtypeStruct((t, qw), BF16),
        ],
        compiler_params=_cparams(("parallel",)),
        name="mla_prep",
    )(z, z, kr, q_norm, kv_norm, wuq_p, wukv, cos_t, sin_t)


RPB_ROWS = 2 * NA_WIN_ROWS - 1
RPB_COLS = 2 * NA_WIN_COLS - 1
NA_Q_ROWS = 4
NA_K_ROWS = NA_Q_ROWS + NA_WIN_ROWS
NA_VARIANTS = 3


def _na_variant_geometry(var, j):
    lo = (0, j, NA_K_ROWS - NA_WIN_ROWS)[var]
    rho0 = (NA_WIN_ROWS - 1) - j - (NA_WIN_ROWS // 2) * var
    return lo, rho0


def _na_bias_body(rpb_ref, o_ref, pair_ref):
    h = pl.program_id(0)
    kw = NA_WIN_COLS
    shape = (GRID_W, LANES)
    q = lax.broadcasted_iota(jnp.int32, shape, 0)
    lane = lax.broadcasted_iota(jnp.int32, shape, 1)
    first = lane < GRID_W
    k = jnp.where(first, lane, lane - GRID_W)
    col_start = jnp.clip(q - kw // 2, 0, GRID_W - kw)
    valid = (k >= col_start) & (k < col_start + kw)
    dc = jnp.clip(k - q, -(kw - 1), kw - 1) + (kw - 1)
    neg = jnp.full(shape, NEG_INF, F32)

    for e in range(RPB_ROWS + 1):
        base_a = (h * RPB_ROWS + max(e - 1, 0)) * RPB_COLS
        base_b = (h * RPB_ROWS + min(e, RPB_ROWS - 1)) * RPB_COLS
        acc = jnp.zeros(shape, F32)
        for c in range(RPB_COLS):
            val = jnp.where(first, rpb_ref[base_a + c], rpb_ref[base_b + c])
            acc = jnp.where(dc == c, val, acc)
        pair_ref[e] = jnp.where(valid, acc * LOG2E, NEG_INF)

    for var in range(NA_VARIANTS):
        for j in range(NA_Q_ROWS):
            lo, rho0 = _na_variant_geometry(var, j)
            for pair in range(NA_K_ROWS // 2):
                i = 2 * pair
                in_a = lo <= i < lo + NA_WIN_ROWS
                in_b = lo <= i + 1 < lo + NA_WIN_ROWS
                if in_a or in_b:
                    tile = pair_ref[rho0 + i + 1]
                    if not in_a:
                        tile = jnp.where(first, neg, tile)
                    if not in_b:
                        tile = jnp.where(first, tile, neg)
                else:
                    tile = neg
                o_ref[var, 0, j * GRID_W:(j + 1) * GRID_W, pair * LANES:(pair + 1) * LANES] = tile


def _na_bias(rpb_flat):
    nq = NA_Q_ROWS * GRID_W
    nk = NA_K_ROWS * GRID_W
    return pl.pallas_call(
        _na_bias_body,
        grid=(NA_HEADS,),
        in_specs=[pl.BlockSpec(memory_space=pltpu.SMEM)],
        out_specs=pl.BlockSpec((NA_VARIANTS, 1, nq, nk), lambda h: (0, h, 0, 0)),
        out_shape=jax.ShapeDtypeStruct((NA_VARIANTS, NA_HEADS, nq, nk), F32),
        scratch_shapes=[pltpu.VMEM((RPB_ROWS + 1, GRID_W, LANES), F32)],
        compiler_params=_cparams(("arbitrary",)),
        name="na_bias",
    )(rpb_flat)


def _na_key_start(g, rows):
    return jnp.clip(g * NA_Q_ROWS - NA_WIN_ROWS // 2, 0, rows - NA_K_ROWS)


def _na_body(q_ref, k_ref, v_ref, b_ref, o_ref, *, rows):
    g = pl.program_id(1)
    start = pl.multiple_of(_na_key_start(g, rows) * GRID_W, GRID_W)
    kind = jnp.where(g == 0, 0, jnp.where(g == rows // NA_Q_ROWS - 1, 2, 1))
    nkeys = NA_K_ROWS * GRID_W
    c = (NA_HEAD_DIM ** -0.5) * LOG2E
    ones = jnp.ones((nkeys, NA_HEAD_DIM), BF16)
    for h in range(NA_HEADS):
        cs = slice(h * NA_HEAD_DIM, (h + 1) * NA_HEAD_DIM)
        q = q_ref[0, :, cs]
        k = k_ref[0, pl.ds(start, nkeys), cs]
        v = v_ref[0, pl.ds(start, nkeys), cs]
        s = _dot_nt(q, k)
        t = s * c + b_ref[kind, h]
        m = jnp.max(t, axis=-1, keepdims=True)
        p = jnp.exp2(t - m).astype(BF16)
        oe = jnp.dot(p, jnp.concatenate([v, ones], axis=1), preferred_element_type=F32)
        o_ref[0, :, cs] = (oe[:, :NA_HEAD_DIM] / oe[:, NA_HEAD_DIM:]).astype(BF16)


def _na_attn(z3, bias, *, q_blk, k_blk, v_blk):
    b, s, _ = z3.shape
    rows = s // GRID_W
    assert rows % NA_Q_ROWS == 0 and rows >= 2 * NA_K_ROWS - NA_WIN_ROWS
    groups = rows // NA_Q_ROWS
    nq = NA_Q_ROWS * GRID_W
    return pl.pallas_call(
        functools.partial(_na_body, rows=rows),
        grid=(b, groups),
        in_specs=[
            pl.BlockSpec((1, nq, NA_WIDTH), lambda bi, g: (bi, g, q_blk)),
            pl.BlockSpec((1, s, NA_WIDTH), lambda bi, g: (bi, 0, k_blk)),
            pl.BlockSpec((1, s, NA_WIDTH), lambda bi, g: (bi, 0, v_blk)),
            pl.BlockSpec(bias.shape, lambda bi, g: (0, 0, 0, 0), pipeline_mode=pl.Buffered(1)),
        ],
        out_specs=pl.BlockSpec((1, nq, NA_WIDTH), lambda bi, g: (bi, g, 0)),
        out_shape=jax.ShapeDtypeStruct((b, s, NA_WIDTH), BF16),
        compiler_params=_cparams(("parallel", "arbitrary")),
        name="na_attn",
    )(z3, z3, z3, bias)


def _mla_body(*refs, n_side, tq, tk):
    q_ref, k_ref, v_ref = refs[:3]
    o_ref = refs[3 + n_side]
    _run_side_jobs(refs[3:3 + n_side], refs[4 + n_side:], None, None)
    seq = q_ref.shape[1]
    c = (MLA_QK_DIM ** -0.5) * LOG2E
    for qi in range(seq // tq):
        q = q_ref[0, qi * tq:(qi + 1) * tq, :]
        m = acc = None
        for kj in range(seq // tk):
            ks = slice(kj * tk, (kj + 1) * tk)
            s = _dot_nt(q, k_ref[0, ks, :]) * c
            mj = jnp.max(s, axis=-1, keepdims=True)
            m_new = mj if m is None else jnp.maximum(m, mj)
            p = jnp.exp2(s - m_new).astype(BF16)
            pv = jnp.dot(p, v_ref[0, ks, :], preferred_element_type=F32)
            acc = pv if acc is None else acc * jnp.exp2(m - m_new) + pv
            m = m_new
        o_ref[0, qi * tq:(qi + 1) * tq, :] = (acc[:, :MLA_V_DIM] / acc[:, MLA_V_DIM:]).astype(BF16)


def _mla_attn(q3, k3, v3, *, casts=(), tq=512, tk=512):
    b, s, _ = q3.shape
    vw = v3.shape[2] // MLA_HEADS
    side = _side_job_specs(casts, None, None, b * MLA_HEADS, lambda bi, h: bi * MLA_HEADS + h)
    outs = pl.pallas_call(
        functools.partial(_mla_body, n_side=len(side[0]), tq=tq, tk=tk),
        grid=(b, MLA_HEADS),
        in_specs=[
            pl.BlockSpec((1, s, MLA_HEAD_PAD), lambda bi, h: (bi, 0, h)),
            pl.BlockSpec((1, s, MLA_HEAD_PAD), lambda bi, h: (bi, 0, h)),
            pl.BlockSpec((1, s, vw), lambda bi, h: (bi, 0, h)),
        ] + side[0],
        out_specs=[pl.BlockSpec((1, s, MLA_V_DIM), lambda bi, h: (bi, 0, h))] + side[1],
        out_shape=[jax.ShapeDtypeStruct((b, s, MLA_WIDTH), BF16)] + side[2],
        compiler_params=_cparams(("arbitrary", "arbitrary")),
        name="mla_attn",
    )(q3, k3, v3, *side[3])
    return outs[0], list(outs[1:])


def _merge_body(na_ref, mla_ref, ga_ref, gb_ref, h_ref, wa_ref, wb_ref, wo_ref, nw_ref,
                o_ref, xn_ref):
    ya = jnp.dot(na_ref[...], wa_ref[...], preferred_element_type=F32)
    yb = jnp.dot(mla_ref[...], wb_ref[...], preferred_element_type=F32)
    merged = _sigmoid(ga_ref[...].astype(F32)) * ya + _sigmoid(gb_ref[...].astype(F32)) * yb
    h = h_ref[...] + jnp.dot(merged.astype(BF16), wo_ref[...], preferred_element_type=F32)
    o_ref[...] = h
    xn_ref[...] = _rms(h, nw_ref[...]).astype(BF16)


def _merge(na_o, mla_o, z, h, wa, wb, wo, next_norm_w, *, ga_blk, gb_blk, tm=256):
    t, d = h.shape
    return pl.pallas_call(
        _merge_body,
        grid=(t // tm,),
        in_specs=[
            pl.BlockSpec((tm, NA_WIDTH), lambda i: (i, 0)),
            pl.BlockSpec((tm, MLA_WIDTH), lambda i: (i, 0)),
            pl.BlockSpec((tm, d), lambda i: (i, ga_blk)),
            pl.BlockSpec((tm, d), lambda i: (i, gb_blk)),
            pl.BlockSpec((tm, d), lambda i: (i, 0)),
            _resident((NA_WIDTH, d)),
            _resident((MLA_WIDTH, d)),
            _resident((d, d)),
            _resident((1, d)),
        ],
        out_specs=[pl.BlockSpec((tm, d), lambda i: (i, 0))] * 2,
        out_shape=[jax.ShapeDtypeStruct((t, d), F32), jax.ShapeDtypeStruct((t, d), BF16)],
        compiler_params=_cparams(("parallel",)),
        name="merge",
    )(na_o, mla_o, z, z, h, wa, wb, wo, next_norm_w)


def _pl_body(h_ref, p_ref, nw_ref, wg_ref, wp_ref, fn_ref, o_ref, *, final):
    h = h_ref[...]
    xn = _rms(h, nw_ref[...]).astype(BF16)
    gate = _sigmoid(jnp.dot(xn, wg_ref[...], preferred_element_type=F32))
    emb = jnp.dot(p_ref[...].astype(BF16), wp_ref[...], preferred_element_type=F32)
    h = h + gate * emb
    o_ref[...] = _rms(h, fn_ref[...]) if final else h


def _pl_embed(h, p, norm_w, w_gate, w_pl, final_norm, *, final, tm=512):
    t, d = h.shape
    pdim = p.shape[1]
    return pl.pallas_call(
        functools.partial(_pl_body, final=final),
        grid=(t // tm,),
        in_specs=[
            pl.BlockSpec((tm, d), lambda i: (i, 0)),
            pl.BlockSpec((tm, pdim), lambda i: (i, 0)),
            _resident((1, d)),
            _resident((d, d)),
            _resident((pdim, d)),
            _resident((1, d)),
        ],
        out_specs=pl.BlockSpec((tm, d), lambda i: (i, 0)),
        out_shape=jax.ShapeDtypeStruct((t, d), F32),
        compiler_params=_cparams(("parallel",)),
        name="pl_final" if final else "pl_embed",
    )(h, p, norm_w, w_gate, w_pl, final_norm)


def _final_norm_body(h_ref, fn_ref, o_ref):
    o_ref[...] = _rms(h_ref[...], fn_ref[...])


def _rope_tables(seq):
    half = MLA_ROPE_DIM // 2
    pos = jnp.arange(seq, dtype=F32)
    inv_freq = 1.0 / (ROPE_THETA ** (jnp.arange(0, MLA_ROPE_DIM, 2, dtype=F32) / MLA_ROPE_DIM))
    ang = pos[:, None] * inv_freq[None, :]
    cos, sin = jnp.cos(ang), jnp.sin(ang)
    zeros = jnp.zeros((seq, LANES - 2 * half), F32)
    return (jnp.concatenate([cos, cos, zeros], axis=-1),
            jnp.concatenate([-sin, sin, zeros], axis=-1))


def kernel(x, p, ffn1_norm, ffn1_w_gate, ffn1_w_up, ffn1_w_down, mix_norm, w_in, q_a_norm, w_uq, kv_a_norm, w_ukv, na_rpb, w_branch_a, w_branch_b, w_out, ffn2_norm, ffn2_w_gate, ffn2_w_up, ffn2_w_down, pl_norm, w_pl, w_pl_gate, final_norm):
    bsz, seq, d = x.shape
    depth = p.shape[0]
    t = bsz * seq
    bf = lambda w: w.astype(BF16)
    row = lambda g: g.reshape(1, -1).astype(F32)

    c_lat = 3 * NA_WIDTH
    c_rope = c_lat + MLA_Q_RANK + MLA_KV_RANK
    cos_t, sin_t = _rope_tables(seq)

    h = x.reshape(t, d)
    for i in range(depth):
        h, (w_main_t, w_kr_t) = _ffn(
            h, row(ffn1_norm[i]), ffn1_w_gate[i], ffn1_w_up[i], ffn1_w_down[i],
            split_src=jnp.swapaxes(w_in[i], 0, 1), split=(c_rope, MLA_ROPE_DIM))
        z, kr, _ = _in_proj(h, row(mix_norm[i]), w_main_t, w_kr_t)

        wuq = w_uq[i].reshape(MLA_Q_RANK, MLA_HEADS, MLA_QK_DIM)
        wuq_p = bf(jnp.pad(wuq, ((0, 0), (0, 0), (0, MLA_HEAD_PAD - MLA_QK_DIM)))
                   .reshape(MLA_Q_RANK, MLA_HEADS * MLA_HEAD_PAD))
        q_m, k_m, v_m = _mla_prep(
            z, kr, row(q_a_norm[i]), row(kv_a_norm[i]), wuq_p, bf(w_ukv[i]), cos_t, sin_t,
            seq=seq, ql_blk=c_lat // MLA_Q_RANK, kvl_blk=(c_lat + MLA_Q_RANK) // MLA_KV_RANK)

        bias = _na_bias(na_rpb[i].reshape(-1).astype(F32))
        z3 = z.reshape(bsz, seq, z.shape[1])
        na_o = _na_attn(z3, bias, q_blk=0, k_blk=1, v_blk=2)
        later = [ffn2_w_gate[i], ffn2_w_up[i], ffn2_w_down[i], w_branch_a[i], w_branch_b[i],
                 w_out[i], w_pl_gate[i]]
        mla_o, (w2_gate, w2_up, w2_down, wa, wb, wo, w_plg) = _mla_attn(
            q_m.reshape(bsz, seq, -1), k_m.reshape(bsz, seq, -1), v_m.reshape(bsz, seq, -1),
            casts=later)

        gate_col = c_rope
        h, xn2 = _merge(na_o.reshape(t, NA_WIDTH), mla_o.reshape(t, MLA_WIDTH), z, h, wa, wb, wo,
                        row(ffn2_norm[i]), ga_blk=gate_col // d, gb_blk=gate_col // d + 1)

        h, _ = _ffn(h, row(ffn2_norm[i]), w2_gate, w2_up, w2_down, xn=xn2)

        h = _pl_embed(h, p[i].reshape(t, -1), row(pl_norm[i]), w_plg, bf(w_pl[i]),
                      row(final_norm), final=(i == depth - 1))

    if depth == 0:
        tm = 512
        h = pl.pallas_call(
            _final_norm_body,
            grid=(t // tm,),
            in_specs=[pl.BlockSpec((tm, d), lambda i: (i, 0)),
                      pl.BlockSpec((1, d), lambda i: (0, 0))],
            out_specs=pl.BlockSpec((tm, d), lambda i: (i, 0)),
            out_shape=jax.ShapeDtypeStruct((t, d), F32),
            compiler_params=_cparams(("parallel",)),
            name="final_norm",
        )(h, row(final_norm))
    return h.reshape(bsz, seq, d)
```

```python
import functools
import math

import jax
import jax.numpy as jnp
from jax import lax
from jax.experimental import pallas as pl
from jax.experimental.pallas import tpu as pltpu

F32 = jnp.float32
BF16 = jnp.bfloat16

GRID_W = 64
NA_HEADS = 8
NA_HEAD_DIM = 128
NA_WIN_ROWS = 8
NA_WIN_COLS = 16
NA_WIDTH = NA_HEADS * NA_HEAD_DIM
MLA_HEADS = 8
MLA_Q_RANK = 512
MLA_KV_RANK = 512
MLA_NOPE_DIM = 128
MLA_ROPE_DIM = 64
MLA_V_DIM = 128
MLA_QK_DIM = MLA_NOPE_DIM + MLA_ROPE_DIM
MLA_WIDTH = MLA_HEADS * MLA_V_DIM
ROPE_THETA = 10000.0
NORM_EPS = 1e-6
NEG_INF = -1e30
LOG2E = math.log2(math.e)

LANES = 128
BF16_ROWS = 16
MLA_HEAD_PAD = 2 * LANES
VMEM_LIMIT_BYTES = 56 * 1024 * 1024


def _cparams(sem):
    return pltpu.CompilerParams(dimension_semantics=sem, vmem_limit_bytes=VMEM_LIMIT_BYTES)


def _resident(shape):
    return pl.BlockSpec(shape, lambda *_: (0, 0), pipeline_mode=pl.Buffered(1))


def _rms(x, g):
    ms = jnp.mean(x * x, axis=-1, keepdims=True)
    return x * lax.rsqrt(ms + NORM_EPS) * g


def _sigmoid(x):
    return 1.0 / (1.0 + jnp.exp(-x))


def _dot_nt(a, b):
    return lax.dot_general(a, b, (((1,), (1,)), ((), ())), preferred_element_type=F32)


def _chunk_rows(rows, nsteps):
    rc = BF16_ROWS
    while rows % rc or rows // rc > nsteps:
        rc += BF16_ROWS
    return rc


def _side_job_specs(casts, split_src, split, nsteps, lin):
    def spec(rows, cols):
        rc = _chunk_rows(rows, nsteps)
        last = rows // rc - 1
        return pl.BlockSpec((rc, cols), lambda *g: (jnp.minimum(lin(*g), last), 0))

    in_specs, out_specs, out_shape, operands = [], [], [], []
    for w in casts:
        in_specs.append(spec(*w.shape))
        out_specs.append(spec(*w.shape))
        out_shape.append(jax.ShapeDtypeStruct(w.shape, BF16))
        operands.append(w)
    if split_src is not None:
        rows, cols = split_src.shape
        keep, skip = split
        assert keep % skip == 0 and rows % skip == 0 and rows // skip <= nsteps and skip <= LANES
        last, hole = rows // skip - 1, keep // skip

        def packed_idx(*g):
            c = jnp.minimum(lin(*g), last)
            return (jnp.where(c <= hole, jnp.minimum(c, hole - 1), c - 1), 0)

        in_specs.append(pl.BlockSpec((skip, cols), lambda *g: (jnp.minimum(lin(*g), last), 0)))
        out_specs += [pl.BlockSpec((skip, cols), packed_idx),
                      pl.BlockSpec((LANES, cols), lambda *g: (0, 0))]
        out_shape += [jax.ShapeDtypeStruct((rows - skip, cols), BF16),
                      jax.ShapeDtypeStruct((LANES, cols), BF16)]
        operands.append(split_src)
    return in_specs, out_specs, out_shape, operands


def _run_side_jobs(in_refs, out_refs, split, step):
    n_cast = len(in_refs) - (1 if split else 0)
    for src, dst in zip(in_refs[:n_cast], out_refs[:n_cast]):
        dst[...] = src[...].astype(BF16)
    if split:
        keep, skip = split
        src = in_refs[n_cast]
        packed_ref, moved_ref = out_refs[n_cast:]
        hole = keep // skip

        @pl.when(step != hole)
        def _():
            packed_ref[...] = src[...].astype(BF16)

        @pl.when(step == hole)
        def _():
            moved_ref[:skip, :] = src[...].astype(BF16)
            if skip < LANES:
                moved_ref[skip:, :] = jnp.zeros((LANES - skip, src.shape[1]), BF16)


FFN_UP_ROWS = 1024


def _ffn_up_body(*refs, n_side, split, prenormed):
    x_ref, nw_ref, wg_ref, wu_ref = refs[:4]
    side_in = refs[4:4 + n_side]
    a_ref = refs[4 + n_side]
    j = pl.program_id(1)
    if prenormed:
        side_out = refs[5 + n_side:]
        xn_ref = x_ref
    else:
        side_out = refs[5 + n_side:-1]
        xn_ref = refs[-1]

        @pl.when(j == 0)
        def _():
            xn_ref[...] = _rms(x_ref[...], nw_ref[...]).astype(BF16)

    wg = wg_ref[...].astype(BF16)
    wu = wu_ref[...].astype(BF16)
    rows_per = min(xn_ref.shape[0], FFN_UP_ROWS)
    for r in range(xn_ref.shape[0] // rows_per):
        rows = slice(r * rows_per, (r + 1) * rows_per)
        xn = xn_ref[rows, :]
        g = jnp.dot(xn, wg, preferred_element_type=F32)
        u = jnp.dot(xn, wu, preferred_element_type=F32)
        a_ref[rows, :] = (g * _sigmoid(g) * u * 0.5).astype(BF16)
    _run_side_jobs(side_in, side_out, split, pl.program_id(0) * pl.num_programs(1) + j)


def _ffn_down_body(a_ref, wd_ref, x_ref, nw_ref, o_ref, *xn_ref):
    h = x_ref[...] + jnp.dot(a_ref[...], wd_ref[...], preferred_element_type=F32)
    o_ref[...] = h
    if xn_ref:
        xn_ref[0][...] = _rms(h, nw_ref[...]).astype(BF16)


def _ffn(x, norm_w, wg, wu, wd, *, xn=None, next_norm_w=None, casts=(), split_src=None, split=None,
         tm=1024, tf=512, tm_down=256):
    t, d = x.shape
    f = wg.shape[1]
    nj = f // tf
    casts = list(casts)
    if wd.dtype != BF16:
        casts.append(wd)
    prenormed = xn is not None
    tm_up = 2 * tm if prenormed else tm
    side = _side_job_specs(casts, split_src, split, (t // tm_up) * nj, lambda i, j: i * nj + j)
    outs = pl.pallas_call(
        functools.partial(_ffn_up_body, n_side=len(side[0]), prenormed=prenormed,
                          split=split if split_src is not None else None),
        grid=(t // tm_up, nj),
        in_specs=[
            pl.BlockSpec((tm_up, d), lambda i, j: (i, 0)),
            pl.BlockSpec((1, d), lambda i, j: (0, 0)),
            pl.BlockSpec((d, tf), lambda i, j: (0, j)),
            pl.BlockSpec((d, tf), lambda i, j: (0, j)),
        ] + side[0],
        out_specs=[pl.BlockSpec((tm_up, tf), lambda i, j: (i, j))] + side[1],
        out_shape=[jax.ShapeDtypeStruct((t, f), BF16)] + side[2],
        scratch_shapes=[] if prenormed else [pltpu.VMEM((tm, d), BF16)],
        compiler_params=_cparams(("arbitrary", "arbitrary")),
        name="ffn_up",
    )(xn if prenormed else x, norm_w, wg, wu, *side[3])
    a = outs[0]
    converted = list(outs[1:])
    if wd.dtype != BF16:
        wd = converted.pop(len(casts) - 1)
    emit_norm = next_norm_w is not None
    row_spec = pl.BlockSpec((tm_down, d), lambda i: (i, 0))
    outs = pl.pallas_call(
        _ffn_down_body,
        grid=(t // tm_down,),
        in_specs=[
            pl.BlockSpec((tm_down, f), lambda i: (i, 0)),
            _resident((f, d)),
            row_spec,
            _resident((1, d)),
        ],
        out_specs=[row_spec] * (2 if emit_norm else 1),
        out_shape=[jax.ShapeDtypeStruct((t, d), F32)]
        + ([jax.ShapeDtypeStruct((t, d), BF16)] if emit_norm else []),
        compiler_params=_cparams(("parallel",)),
        name="ffn_down",
    )(a, wd, x, next_norm_w if emit_norm else norm_w)
    return (outs[0], outs[1] if emit_norm else None), converted


IN_PROJ_ROWS = 1024


def _in_proj_body(xn_ref, w_ref, wkr_ref, z_ref, kr_ref):
    @pl.when(pl.program_id(1) == 0)
    def _():
        kr_ref[...] = _dot_nt(xn_ref[...], wkr_ref[...])

    rows_per = min(xn_ref.shape[0], IN_PROJ_ROWS)
    for r in range(xn_ref.shape[0] // rows_per):
        rows = slice(r * rows_per, (r + 1) * rows_per)
        z_ref[rows, :] = _dot_nt(xn_ref[rows, :], w_ref[...]).astype(BF16)


def _in_proj(xn, w_main_t, w_kr_t, *, tm=2048, tn=1024):
    t, d = xn.shape
    n = w_main_t.shape[0]
    return pl.pallas_call(
        _in_proj_body,
        grid=(t // tm, n // tn),
        in_specs=[
            pl.BlockSpec((tm, d), lambda i, j: (i, 0)),
            pl.BlockSpec((tn, d), lambda i, j: (j, 0)),
            pl.BlockSpec((LANES, d), lambda i, j: (0, 0)),
        ],
        out_specs=[
            pl.BlockSpec((tm, tn), lambda i, j: (i, j)),
            pl.BlockSpec((tm, LANES), lambda i, j: (i, 0)),
        ],
        out_shape=[
            jax.ShapeDtypeStruct((t, n), BF16),
            jax.ShapeDtypeStruct((t, LANES), F32),
        ],
        compiler_params=_cparams(("parallel", "arbitrary")),
        name="in_proj",
    )(xn, w_main_t, w_kr_t)


def _rope(r, cos_t, sin_t):
    lane = lax.broadcasted_iota(jnp.int32, r.shape, 1)
    half = MLA_ROPE_DIM // 2
    swapped = jnp.where(lane < half, pltpu.roll(r, LANES - half, 1), pltpu.roll(r, half, 1))
    return r * cos_t + swapped * sin_t


def _mla_prep_body(ql_ref, kvl_ref, kr_ref, qn_ref, kvn_ref, wuq_ref, wukv_ref,
                   cos_ref, sin_ref, q_out, k_out, v_out):
    cq = _rms(ql_ref[...].astype(F32), qn_ref[...]).astype(BF16)
    ckv = _rms(kvl_ref[...].astype(F32), kvn_ref[...]).astype(BF16)
    cos_t = cos_ref[...]
    sin_t = sin_ref[...]
    k_rope = _rope(kr_ref[...], cos_t, sin_t).astype(BF16)
    for h in range(MLA_HEADS):
        lo = h * MLA_HEAD_PAD
        mid = lo + LANES
        hi = lo + MLA_HEAD_PAD
        q = jnp.dot(cq, wuq_ref[:, lo:hi], preferred_element_type=F32)
        kv = jnp.dot(ckv, wukv_ref[:, lo:hi], preferred_element_type=F32)
        q_out[:, lo:mid] = q[:, :LANES].astype(BF16)
        q_out[:, mid:hi] = _rope(q[:, LANES:], cos_t, sin_t).astype(BF16)
        k_out[:, lo:mid] = kv[:, :LANES].astype(BF16)
        k_out[:, mid:hi] = k_rope
        v_out[:, lo:mid] = kv[:, LANES:].astype(BF16)
        v_out[:, mid:hi] = jnp.ones((kv.shape[0], MLA_HEAD_PAD - MLA_V_DIM), BF16)


def _mla_prep(z, kr, q_norm, kv_norm, wuq_p, wukv, cos_t, sin_t, *, seq, ql_blk, kvl_blk, tm=512):
    t = z.shape[0]
    nseq = seq // tm
    qw = MLA_HEADS * MLA_HEAD_PAD
    return pl.pallas_call(
        _mla_prep_body,
        grid=(t // tm,),
        in_specs=[
            pl.BlockSpec((tm, MLA_Q_RANK), lambda i: (i, ql_blk)),
            pl.BlockSpec((tm, MLA_KV_RANK), lambda i: (i, kvl_blk)),
            pl.BlockSpec((tm, LANES), lambda i: (i, 0)),
            pl.BlockSpec((1, MLA_Q_RANK), lambda i: (0, 0)),
            pl.BlockSpec((1, MLA_KV_RANK), lambda i: (0, 0)),
            pl.BlockSpec((MLA_Q_RANK, qw), lambda i: (0, 0)),
            pl.BlockSpec((MLA_KV_RANK, qw), lambda i: (0, 0)),
            pl.BlockSpec((tm, LANES), lambda i: (i % nseq, 0)),
            pl.BlockSpec((tm, LANES), lambda i: (i % nseq, 0)),
        ],
        out_specs=[
            pl.BlockSpec((tm, qw), lambda i: (i, 0)),
            pl.BlockSpec((tm, qw), lambda i: (i, 0)),
            pl.BlockSpec((tm, qw), lambda i: (i, 0)),
        ],
        out_shape=[
            jax.ShapeDtypeStruct((t, qw), BF16),
            jax.ShapeDtypeStruct((t, qw), BF16),
            jax.ShapeDtypeStruct((t, qw), BF16),
        ],
        compiler_params=_cparams(("parallel",)),
        name="mla_prep",
    )(z, z, kr, q_norm, kv_norm, wuq_p, wukv, cos_t, sin_t)


RPB_ROWS = 2 * NA_WIN_ROWS - 1
RPB_COLS = 2 * NA_WIN_COLS - 1
NA_Q_ROWS = 4
NA_K_ROWS = NA_Q_ROWS + NA_WIN_ROWS
NA_VARIANTS = 3


def _na_variant_geometry(var, j):
    lo = (0, j, NA_K_ROWS - NA_WIN_ROWS)[var]
    rho0 = (NA_WIN_ROWS - 1) - j - (NA_WIN_ROWS // 2) * var
    return lo, rho0


def _na_bias_body(rpb_ref, o_ref, pair_ref):
    h = pl.program_id(0)
    kw = NA_WIN_COLS
    shape = (GRID_W, LANES)
    q = lax.broadcasted_iota(jnp.int32, shape, 0)
    lane = lax.broadcasted_iota(jnp.int32, shape, 1)
    first = lane < GRID_W
    k = jnp.where(first, lane, lane - GRID_W)
    col_start = jnp.clip(q - kw // 2, 0, GRID_W - kw)
    valid = (k >= col_start) & (k < col_start + kw)
    dc = jnp.clip(k - q, -(kw - 1), kw - 1) + (kw - 1)
    neg = jnp.full(shape, NEG_INF, F32)

    for e in range(RPB_ROWS + 1):
        base_a = (h * RPB_ROWS + max(e - 1, 0)) * RPB_COLS
        base_b = (h * RPB_ROWS + min(e, RPB_ROWS - 1)) * RPB_COLS
        acc = jnp.zeros(shape, F32)
        for c in range(RPB_COLS):
            val = jnp.where(first, rpb_ref[base_a + c], rpb_ref[base_b + c])
            acc = jnp.where(dc == c, val, acc)
        pair_ref[e] = jnp.where(valid, acc * LOG2E, NEG_INF)

    for var in range(NA_VARIANTS):
        for j in range(NA_Q_ROWS):
            lo, rho0 = _na_variant_geometry(var, j)
            for pair in range(NA_K_ROWS // 2):
                i = 2 * pair
                in_a = lo <= i < lo + NA_WIN_ROWS
                in_b = lo <= i + 1 < lo + NA_WIN_ROWS
                if in_a or in_b:
                    tile = pair_ref[rho0 + i + 1]
                    if not in_a:
                        tile = jnp.where(first, neg, tile)
                    if not in_b:
                        tile = jnp.where(first, tile, neg)
                else:
                    tile = neg
                o_ref[var, 0, j * GRID_W:(j + 1) * GRID_W, pair * LANES:(pair + 1) * LANES] = tile


def _na_bias(rpb_flat):
    nq = NA_Q_ROWS * GRID_W
    nk = NA_K_ROWS * GRID_W
    return pl.pallas_call(
        _na_bias_body,
        grid=(NA_HEADS,),
        in_specs=[pl.BlockSpec(memory_space=pltpu.SMEM)],
        out_specs=pl.BlockSpec((NA_VARIANTS, 1, nq, nk), lambda h: (0, h, 0, 0)),
        out_shape=jax.ShapeDtypeStruct((NA_VARIANTS, NA_HEADS, nq, nk), F32),
        scratch_shapes=[pltpu.VMEM((RPB_ROWS + 1, GRID_W, LANES), F32)],
        compiler_params=_cparams(("arbitrary",)),
        name="na_bias",
    )(rpb_flat)


def _na_key_start(g, rows):
    return jnp.clip(g * NA_Q_ROWS - NA_WIN_ROWS // 2, 0, rows - NA_K_ROWS)


def _na_body(q_ref, k_ref, v_ref, b_ref, o_ref, *, rows):
    g = pl.program_id(1)
    start = pl.multiple_of(_na_key_start(g, rows) * GRID_W, GRID_W)
    kind = jnp.where(g == 0, 0, jnp.where(g == rows // NA_Q_ROWS - 1, 2, 1))
    nkeys = NA_K_ROWS * GRID_W
    c = (NA_HEAD_DIM ** -0.5) * LOG2E
    ones = jnp.ones((nkeys, NA_HEAD_DIM), BF16)
    for h in range(NA_HEADS):
        cs = slice(h * NA_HEAD_DIM, (h + 1) * NA_HEAD_DIM)
        q = q_ref[0, :, cs]
        k = k_ref[0, pl.ds(start, nkeys), cs]
        v = v_ref[0, pl.ds(start, nkeys), cs]
        s = _dot_nt(q, k)
        t = s * c + b_ref[kind, h]
        m = jnp.max(t, axis=-1, keepdims=True)
        p = jnp.exp2(t - m).astype(BF16)
        oe = jnp.dot(p, jnp.concatenate([v, ones], axis=1), preferred_element_type=F32)
        o_ref[0, :, cs] = (oe[:, :NA_HEAD_DIM] / oe[:, NA_HEAD_DIM:]).astype(BF16)


def _na_attn(z3, bias, *, q_blk, k_blk, v_blk):
    b, s, _ = z3.shape
    rows = s // GRID_W
    assert rows % NA_Q_ROWS == 0 and rows >= 2 * NA_K_ROWS - NA_WIN_ROWS
    groups = rows // NA_Q_ROWS
    nq = NA_Q_ROWS * GRID_W
    return pl.pallas_call(
        functools.partial(_na_body, rows=rows),
        grid=(b, groups),
        in_specs=[
            pl.BlockSpec((1, nq, NA_WIDTH), lambda bi, g: (bi, g, q_blk)),
            pl.BlockSpec((1, s, NA_WIDTH), lambda bi, g: (bi, 0, k_blk)),
            pl.BlockSpec((1, s, NA_WIDTH), lambda bi, g: (bi, 0, v_blk)),
            pl.BlockSpec(bias.shape, lambda bi, g: (0, 0, 0, 0), pipeline_mode=pl.Buffered(1)),
        ],
        out_specs=pl.BlockSpec((1, nq, NA_WIDTH), lambda bi, g: (bi, g, 0)),
        out_shape=jax.ShapeDtypeStruct((b, s, NA_WIDTH), BF16),
        compiler_params=_cparams(("parallel", "arbitrary")),
        name="na_attn",
    )(z3, z3, z3, bias)


def _mla_body(*refs, n_side, tq, tk):
    q_ref, k_ref, v_ref = refs[:3]
    o_ref = refs[3 + n_side]
    _run_side_jobs(refs[3:3 + n_side], refs[4 + n_side:], None, None)
    seq = q_ref.shape[1]
    c = (MLA_QK_DIM ** -0.5) * LOG2E
    for qi in range(seq // tq):
        q = q_ref[0, qi * tq:(qi + 1) * tq, :]
        m = acc = None
        for kj in range(seq // tk):
            ks = slice(kj * tk, (kj + 1) * tk)
            s = _dot_nt(q, k_ref[0, ks, :]) * c
            mj = jnp.max(s, axis=-1, keepdims=True)
            m_new = mj if m is None else jnp.maximum(m, mj)
            p = jnp.exp2(s - m_new).astype(BF16)
            pv = jnp.dot(p, v_ref[0, ks, :], preferred_element_type=F32)
            acc = pv if acc is None else acc * jnp.exp2(m - m_new) + pv
            m = m_new
        o_ref[0, qi * tq:(qi + 1) * tq, :] = (acc[:, :MLA_V_DIM] / acc[:, MLA_V_DIM:]).astype(BF16)


def _mla_attn(q3, k3, v3, *, casts=(), tq=512, tk=512):
    b, s, _ = q3.shape
    vw = v3.shape[2] // MLA_HEADS
    side = _side_job_specs(casts, None, None, b * MLA_HEADS, lambda bi, h: bi * MLA_HEADS + h)
    outs = pl.pallas_call(
        functools.partial(_mla_body, n_side=len(side[0]), tq=tq, tk=tk),
        grid=(b, MLA_HEADS),
        in_specs=[
            pl.BlockSpec((1, s, MLA_HEAD_PAD), lambda bi, h: (bi, 0, h)),
            pl.BlockSpec((1, s, MLA_HEAD_PAD), lambda bi, h: (bi, 0, h)),
            pl.BlockSpec((1, s, vw), lambda bi, h: (bi, 0, h)),
        ] + side[0],
        out_specs=[pl.BlockSpec((1, s, MLA_V_DIM), lambda bi, h: (bi, 0, h))] + side[1],
        out_shape=[jax.ShapeDtypeStruct((b, s, MLA_WIDTH), BF16)] + side[2],
        compiler_params=_cparams(("arbitrary", "arbitrary")),
        name="mla_attn",
    )(q3, k3, v3, *side[3])
    return outs[0], list(outs[1:])


def _merge_body(na_ref, mla_ref, ga_ref, gb_ref, h_ref, wa_ref, wb_ref, wo_ref, nw_ref,
                o_ref, xn_ref):
    ya = jnp.dot(na_ref[...], wa_ref[...], preferred_element_type=F32)
    yb = jnp.dot(mla_ref[...], wb_ref[...], preferred_element_type=F32)
    merged = _sigmoid(ga_ref[...].astype(F32)) * ya + _sigmoid(gb_ref[...].astype(F32)) * yb
    h = h_ref[...] + jnp.dot(merged.astype(BF16), wo_ref[...], preferred_element_type=F32)
    o_ref[...] = h
    xn_ref[...] = _rms(h, nw_ref[...]).astype(BF16)


def _merge(na_o, mla_o, z, h, wa, wb, wo, next_norm_w, *, ga_blk, gb_blk, tm=256):
    t, d = h.shape
    return pl.pallas_call(
        _merge_body,
        grid=(t // tm,),
        in_specs=[
            pl.BlockSpec((tm, NA_WIDTH), lambda i: (i, 0)),
            pl.BlockSpec((tm, MLA_WIDTH), lambda i: (i, 0)),
            pl.BlockSpec((tm, d), lambda i: (i, ga_blk)),
            pl.BlockSpec((tm, d), lambda i: (i, gb_blk)),
            pl.BlockSpec((tm, d), lambda i: (i, 0)),
            _resident((NA_WIDTH, d)),
            _resident((MLA_WIDTH, d)),
            _resident((d, d)),
            _resident((1, d)),
        ],
        out_specs=[pl.BlockSpec((tm, d), lambda i: (i, 0))] * 2,
        out_shape=[jax.ShapeDtypeStruct((t, d), F32), jax.ShapeDtypeStruct((t, d), BF16)],
        compiler_params=_cparams(("parallel",)),
        name="merge",
    )(na_o, mla_o, z, z, h, wa, wb, wo, next_norm_w)


def _pl_body(h_ref, p_ref, nw_ref, wg_ref, wp_ref, fn_ref, o_ref, *, final):
    h = h_ref[...]
    xn = _rms(h, nw_ref[...]).astype(BF16)
    gate = _sigmoid(jnp.dot(xn, wg_ref[...], preferred_element_type=F32))
    emb = jnp.dot(p_ref[...].astype(BF16), wp_ref[...], preferred_element_type=F32)
    h = h + gate * emb
    o_ref[...] = _rms(h, fn_ref[...]) if final else h


def _pl_embed(h, p, norm_w, w_gate, w_pl, final_norm, *, final, tm=512):
    t, d = h.shape
    pdim = p.shape[1]
    return pl.pallas_call(
        functools.partial(_pl_body, final=final),
        grid=(t // tm,),
        in_specs=[
            pl.BlockSpec((tm, d), lambda i: (i, 0)),
            pl.BlockSpec((tm, pdim), lambda i: (i, 0)),
            _resident((1, d)),
            _resident((d, d)),
            _resident((pdim, d)),
            _resident((1, d)),
        ],
        out_specs=pl.BlockSpec((tm, d), lambda i: (i, 0)),
        out_shape=jax.ShapeDtypeStruct((t, d), F32),
        compiler_params=_cparams(("parallel",)),
        name="pl_final" if final else "pl_embed",
    )(h, p, norm_w, w_gate, w_pl, final_norm)


def _final_norm_body(h_ref, fn_ref, o_ref):
    o_ref[...] = _rms(h_ref[...], fn_ref[...])


def _rope_tables(seq):
    half = MLA_ROPE_DIM // 2
    pos = jnp.arange(seq, dtype=F32)
    inv_freq = 1.0 / (ROPE_THETA ** (jnp.arange(0, MLA_ROPE_DIM, 2, dtype=F32) / MLA_ROPE_DIM))
    ang = pos[:, None] * inv_freq[None, :]
    cos, sin = jnp.cos(ang), jnp.sin(ang)
    zeros = jnp.zeros((seq, LANES - 2 * half), F32)
    return (jnp.concatenate([cos, cos, zeros], axis=-1),
            jnp.concatenate([-sin, sin, zeros], axis=-1))


def kernel(x, p, ffn1_norm, ffn1_w_gate, ffn1_w_up, ffn1_w_down, mix_norm, w_in, q_a_norm, w_uq, kv_a_norm, w_ukv, na_rpb, w_branch_a, w_branch_b, w_out, ffn2_norm, ffn2_w_gate, ffn2_w_up, ffn2_w_down, pl_norm, w_pl, w_pl_gate, final_norm):
    bsz, seq, d = x.shape
    depth = p.shape[0]
    t = bsz * seq
    bf = lambda w: w.astype(BF16)
    row = lambda g: g.reshape(1, -1).astype(F32)

    c_lat = 3 * NA_WIDTH
    c_rope = c_lat + MLA_Q_RANK + MLA_KV_RANK
    cos_t, sin_t = _rope_tables(seq)

    h = x.reshape(t, d)
    for i in range(depth):
        (h, xn_mix), (w_main_t, w_kr_t) = _ffn(
            h, row(ffn1_norm[i]), ffn1_w_gate[i], ffn1_w_up[i], ffn1_w_down[i],
            next_norm_w=row(mix_norm[i]),
            split_src=jnp.swapaxes(w_in[i], 0, 1), split=(c_rope, MLA_ROPE_DIM))
        z, kr = _in_proj(xn_mix, w_main_t, w_kr_t)

        wuq = w_uq[i].reshape(MLA_Q_RANK, MLA_HEADS, MLA_QK_DIM)
        wuq_p = bf(jnp.pad(wuq, ((0, 0), (0, 0), (0, MLA_HEAD_PAD - MLA_QK_DIM)))
                   .reshape(MLA_Q_RANK, MLA_HEADS * MLA_HEAD_PAD))
        q_m, k_m, v_m = _mla_prep(
            z, kr, row(q_a_norm[i]), row(kv_a_norm[i]), wuq_p, bf(w_ukv[i]), cos_t, sin_t,
            seq=seq, ql_blk=c_lat // MLA_Q_RANK, kvl_blk=(c_lat + MLA_Q_RANK) // MLA_KV_RANK)

        bias = _na_bias(na_rpb[i].reshape(-1).astype(F32))
        z3 = z.reshape(bsz, seq, z.shape[1])
        na_o = _na_attn(z3, bias, q_blk=0, k_blk=1, v_blk=2)
        later = [ffn2_w_gate[i], ffn2_w_up[i], ffn2_w_down[i], w_branch_a[i], w_branch_b[i],
                 w_out[i], w_pl_gate[i]]
        mla_o, (w2_gate, w2_up, w2_down, wa, wb, wo, w_plg) = _mla_attn(
            q_m.reshape(bsz, seq, -1), k_m.reshape(bsz, seq, -1), v_m.reshape(bsz, seq, -1),
            casts=later)

        gate_col = c_rope
        h, xn2 = _merge(na_o.reshape(t, NA_WIDTH), mla_o.reshape(t, MLA_WIDTH), z, h, wa, wb, wo,
                        row(ffn2_norm[i]), ga_blk=gate_col // d, gb_blk=gate_col // d + 1)

        (h, _), _ = _ffn(h, row(ffn2_norm[i]), w2_gate, w2_up, w2_down, xn=xn2)

        h = _pl_embed(h, p[i].reshape(t, -1), row(pl_norm[i]), w_plg, bf(w_pl[i]),
                      row(final_norm), final=(i == depth - 1))

    if depth == 0:
        tm = 512
        h = pl.pallas_call(
            _final_norm_body,
            grid=(t // tm,),
            in_specs=[pl.BlockSpec((tm, d), lambda i: (i, 0)),
                      pl.BlockSpec((1, d), lambda i: (0, 0))],
            out_specs=pl.BlockSpec((tm, d), lambda i: (i, 0)),
            out_shape=jax.ShapeDtypeStruct((t, d), F32),
            compiler_params=_cparams(("parallel",)),
            name="final_norm",
        )(h, row(final_norm))
    return h.reshape(bsz, seq, d)
```

```python
import functools
import math

import jax
import jax.numpy as jnp
from jax import lax
from jax.experimental import pallas as pl
from jax.experimental.pallas import tpu as pltpu

F32 = jnp.float32
BF16 = jnp.bfloat16

GRID_W = 64
NA_HEADS = 8
NA_HEAD_DIM = 128
NA_WIN_ROWS = 8
NA_WIN_COLS = 16
NA_WIDTH = NA_HEADS * NA_HEAD_DIM
MLA_HEADS = 8
MLA_Q_RANK = 512
MLA_KV_RANK = 512
MLA_NOPE_DIM = 128
MLA_ROPE_DIM = 64
MLA_V_DIM = 128
MLA_QK_DIM = MLA_NOPE_DIM + MLA_ROPE_DIM
MLA_WIDTH = MLA_HEADS * MLA_V_DIM
ROPE_THETA = 10000.0
NORM_EPS = 1e-6
NEG_INF = -1e30
LOG2E = math.log2(math.e)

LANES = 128
BF16_ROWS = 16
MLA_HEAD_PAD = 2 * LANES
VMEM_LIMIT_BYTES = 56 * 1024 * 1024


def _cparams(sem):
    return pltpu.CompilerParams(dimension_semantics=sem, vmem_limit_bytes=VMEM_LIMIT_BYTES)


def _resident(shape):
    return pl.BlockSpec(shape, lambda *_: (0, 0), pipeline_mode=pl.Buffered(1))


def _rms(x, g):
    ms = jnp.mean(x * x, axis=-1, keepdims=True)
    return x * lax.rsqrt(ms + NORM_EPS) * g


def _sigmoid(x):
    return 1.0 / (1.0 + jnp.exp(-x))


def _dot_nt(a, b):
    return lax.dot_general(a, b, (((1,), (1,)), ((), ())), preferred_element_type=F32)


def _chunk_rows(rows, nsteps):
    rc = BF16_ROWS
    while rows % rc or rows // rc > nsteps:
        rc += BF16_ROWS
    return rc


def _side_job_specs(casts, split_src, split, nsteps, lin):
    def spec(rows, cols):
        rc = _chunk_rows(rows, nsteps)
        last = rows // rc - 1
        return pl.BlockSpec((rc, cols), lambda *g: (jnp.minimum(lin(*g), last), 0))

    in_specs, out_specs, out_shape, operands = [], [], [], []
    for w in casts:
        in_specs.append(spec(*w.shape))
        out_specs.append(spec(*w.shape))
        out_shape.append(jax.ShapeDtypeStruct(w.shape, BF16))
        operands.append(w)
    if split_src is not None:
        rows, cols = split_src.shape
        keep, skip = split
        assert keep % skip == 0 and rows % skip == 0 and rows // skip <= nsteps and skip <= LANES
        last, hole = rows // skip - 1, keep // skip

        def packed_idx(*g):
            c = jnp.minimum(lin(*g), last)
            return (jnp.where(c <= hole, jnp.minimum(c, hole - 1), c - 1), 0)

        in_specs.append(pl.BlockSpec((skip, cols), lambda *g: (jnp.minimum(lin(*g), last), 0)))
        out_specs += [pl.BlockSpec((skip, cols), packed_idx),
                      pl.BlockSpec((LANES, cols), lambda *g: (0, 0))]
        out_shape += [jax.ShapeDtypeStruct((rows - skip, cols), BF16),
                      jax.ShapeDtypeStruct((LANES, cols), BF16)]
        operands.append(split_src)
    return in_specs, out_specs, out_shape, operands


def _run_side_jobs(in_refs, out_refs, split, step):
    n_cast = len(in_refs) - (1 if split else 0)
    for src, dst in zip(in_refs[:n_cast], out_refs[:n_cast]):
        dst[...] = src[...].astype(BF16)
    if split:
        keep, skip = split
        src = in_refs[n_cast]
        packed_ref, moved_ref = out_refs[n_cast:]
        hole = keep // skip

        @pl.when(step != hole)
        def _():
            packed_ref[...] = src[...].astype(BF16)

        @pl.when(step == hole)
        def _():
            moved_ref[:skip, :] = src[...].astype(BF16)
            if skip < LANES:
                moved_ref[skip:, :] = jnp.zeros((LANES - skip, src.shape[1]), BF16)


FFN_UP_ROWS = 1024


def _ffn_up_body(*refs, n_side, split, prenormed):
    x_ref, nw_ref, wg_ref, wu_ref = refs[:4]
    side_in = refs[4:4 + n_side]
    a_ref = refs[4 + n_side]
    j, k = pl.program_id(1), pl.program_id(2)
    if prenormed:
        side_out = refs[5 + n_side:]
        xn_ref = x_ref
    else:
        side_out = refs[5 + n_side:-1]
        xn_ref = refs[-1].at[k]

        @pl.when(j == 0)
        def _():
            xn_ref[...] = _rms(x_ref[...], nw_ref[...]).astype(BF16)

    wg = wg_ref[...].astype(BF16)
    wu = wu_ref[...].astype(BF16)
    rows_per = min(xn_ref.shape[0], FFN_UP_ROWS)
    for r in range(xn_ref.shape[0] // rows_per):
        rows = slice(r * rows_per, (r + 1) * rows_per)
        xn = xn_ref[rows, :]
        g = jnp.dot(xn, wg, preferred_element_type=F32)
        u = jnp.dot(xn, wu, preferred_element_type=F32)
        a_ref[rows, :] = (g * _sigmoid(g) * u * 0.5).astype(BF16)
    step = (pl.program_id(0) * pl.num_programs(1) + j) * pl.num_programs(2) + k
    _run_side_jobs(side_in, side_out, split, step)


def _ffn_down_body(a_ref, wd_ref, x_ref, nw_ref, o_ref, *xn_ref):
    h = x_ref[...] + jnp.dot(a_ref[...], wd_ref[...], preferred_element_type=F32)
    o_ref[...] = h
    if xn_ref:
        xn_ref[0][...] = _rms(h, nw_ref[...]).astype(BF16)


def _ffn(x, norm_w, wg, wu, wd, *, xn=None, next_norm_w=None, casts=(), split_src=None, split=None,
         tm=1024, tf=512, tm_down=256):
    t, d = x.shape
    f = wg.shape[1]
    nj = f // tf
    casts = list(casts)
    if wd.dtype != BF16:
        casts.append(wd)
    prenormed = xn is not None
    tm_up = 2 * tm if prenormed else tm
    ntiles = t // tm_up
    pair = 2 if not prenormed and ntiles % 2 == 0 else 1
    tile = lambda p, k: p * pair + k
    x_idx = (lambda p, j, k: (tile(p, k), 0)) if prenormed else (
        lambda p, j, k: (jnp.where(j == 0, tile(p, k), tile(p, pair - 1)), 0))
    side = _side_job_specs(casts, split_src, split, ntiles * nj,
                           lambda p, j, k: (p * nj + j) * pair + k)
    outs = pl.pallas_call(
        functools.partial(_ffn_up_body, n_side=len(side[0]), prenormed=prenormed,
                          split=split if split_src is not None else None),
        grid=(ntiles // pair, nj, pair),
        in_specs=[
            pl.BlockSpec((tm_up, d), x_idx),
            pl.BlockSpec((1, d), lambda p, j, k: (0, 0)),
            pl.BlockSpec((d, tf), lambda p, j, k: (0, j)),
            pl.BlockSpec((d, tf), lambda p, j, k: (0, j)),
        ] + side[0],
        out_specs=[pl.BlockSpec((tm_up, tf), lambda p, j, k: (tile(p, k), j))] + side[1],
        out_shape=[jax.ShapeDtypeStruct((t, f), BF16)] + side[2],
        scratch_shapes=[] if prenormed else [pltpu.VMEM((pair, tm, d), BF16)],
        compiler_params=_cparams(("arbitrary", "arbitrary", "arbitrary")),
        name="ffn_up",
    )(xn if prenormed else x, norm_w, wg, wu, *side[3])
    a = outs[0]
    converted = list(outs[1:])
    if wd.dtype != BF16:
        wd = converted.pop(len(casts) - 1)
    emit_norm = next_norm_w is not None
    row_spec = pl.BlockSpec((tm_down, d), lambda i: (i, 0))
    outs = pl.pallas_call(
        _ffn_down_body,
        grid=(t // tm_down,),
        in_specs=[
            pl.BlockSpec((tm_down, f), lambda i: (i, 0)),
            _resident((f, d)),
            row_spec,
            _resident((1, d)),
        ],
        out_specs=[row_spec] * (2 if emit_norm else 1),
        out_shape=[jax.ShapeDtypeStruct((t, d), F32)]
        + ([jax.ShapeDtypeStruct((t, d), BF16)] if emit_norm else []),
        compiler_params=_cparams(("parallel",)),
        name="ffn_down",
    )(a, wd, x, next_norm_w if emit_norm else norm_w)
    return (outs[0], outs[1] if emit_norm else None), converted


IN_PROJ_ROWS = 1024


def _in_proj_body(xn_ref, w_ref, wkr_ref, z_ref, kr_ref):
    @pl.when(pl.program_id(1) == 0)
    def _():
        kr_ref[...] = _dot_nt(xn_ref[...], wkr_ref[...])

    rows_per = min(xn_ref.shape[0], IN_PROJ_ROWS)
    for r in range(xn_ref.shape[0] // rows_per):
        rows = slice(r * rows_per, (r + 1) * rows_per)
        z_ref[rows, :] = _dot_nt(xn_ref[rows, :], w_ref[...]).astype(BF16)


def _in_proj(xn, w_main_t, w_kr_t, *, tm=2048, tn=1024):
    t, d = xn.shape
    n = w_main_t.shape[0]
    return pl.pallas_call(
        _in_proj_body,
        grid=(t // tm, n // tn),
        in_specs=[
            pl.BlockSpec((tm, d), lambda i, j: (i, 0)),
            pl.BlockSpec((tn, d), lambda i, j: (j, 0)),
            pl.BlockSpec((LANES, d), lambda i, j: (0, 0)),
        ],
        out_specs=[
            pl.BlockSpec((tm, tn), lambda i, j: (i, j)),
            pl.BlockSpec((tm, LANES), lambda i, j: (i, 0)),
        ],
        out_shape=[
            jax.ShapeDtypeStruct((t, n), BF16),
            jax.ShapeDtypeStruct((t, LANES), F32),
        ],
        compiler_params=_cparams(("parallel", "arbitrary")),
        name="in_proj",
    )(xn, w_main_t, w_kr_t)


def _rope(r, cos_t, sin_t):
    lane = lax.broadcasted_iota(jnp.int32, r.shape, 1)
    half = MLA_ROPE_DIM // 2
    swapped = jnp.where(lane < half, pltpu.roll(r, LANES - half, 1), pltpu.roll(r, half, 1))
    return r * cos_t + swapped * sin_t


def _mla_prep_body(ql_ref, kvl_ref, kr_ref, qn_ref, kvn_ref, wuq_ref, wukv_ref,
                   cos_ref, sin_ref, q_out, k_out, v_out):
    cq = _rms(ql_ref[...].astype(F32), qn_ref[...]).astype(BF16)
    ckv = _rms(kvl_ref[...].astype(F32), kvn_ref[...]).astype(BF16)
    cos_t = cos_ref[...]
    sin_t = sin_ref[...]
    k_rope = _rope(kr_ref[...], cos_t, sin_t).astype(BF16)
    for h in range(MLA_HEADS):
        lo = h * MLA_HEAD_PAD
        mid = lo + LANES
        hi = lo + MLA_HEAD_PAD
        q = jnp.dot(cq, wuq_ref[:, lo:hi], preferred_element_type=F32)
        kv = jnp.dot(ckv, wukv_ref[:, lo:hi], preferred_element_type=F32)
        q_out[:, lo:mid] = q[:, :LANES].astype(BF16)
        q_out[:, mid:hi] = _rope(q[:, LANES:], cos_t, sin_t).astype(BF16)
        k_out[:, lo:mid] = kv[:, :LANES].astype(BF16)
        k_out[:, mid:hi] = k_rope
        v_out[:, lo:mid] = kv[:, LANES:].astype(BF16)
        v_out[:, mid:hi] = jnp.ones((kv.shape[0], MLA_HEAD_PAD - MLA_V_DIM), BF16)


def _mla_prep(z, kr, q_norm, kv_norm, wuq_p, wukv, cos_t, sin_t, *, seq, ql_blk, kvl_blk, tm=512):
    t = z.shape[0]
    nseq = seq // tm
    qw = MLA_HEADS * MLA_HEAD_PAD
    return pl.pallas_call(
        _mla_prep_body,
        grid=(t // tm,),
        in_specs=[
            pl.BlockSpec((tm, MLA_Q_RANK), lambda i: (i, ql_blk)),
            pl.BlockSpec((tm, MLA_KV_RANK), lambda i: (i, kvl_blk)),
            pl.BlockSpec((tm, LANES), lambda i: (i, 0)),
            pl.BlockSpec((1, MLA_Q_RANK), lambda i: (0, 0)),
            pl.BlockSpec((1, MLA_KV_RANK), lambda i: (0, 0)),
            pl.BlockSpec((MLA_Q_RANK, qw), lambda i: (0, 0)),
            pl.BlockSpec((MLA_KV_RANK, qw), lambda i: (0, 0)),
            pl.BlockSpec((tm, LANES), lambda i: (i % nseq, 0)),
            pl.BlockSpec((tm, LANES), lambda i: (i % nseq, 0)),
        ],
        out_specs=[
            pl.BlockSpec((tm, qw), lambda i: (i, 0)),
            pl.BlockSpec((tm, qw), lambda i: (i, 0)),
            pl.BlockSpec((tm, qw), lambda i: (i, 0)),
        ],
        out_shape=[
            jax.ShapeDtypeStruct((t, qw), BF16),
            jax.ShapeDtypeStruct((t, qw), BF16),
            jax.ShapeDtypeStruct((t, qw), BF16),
        ],
        compiler_params=_cparams(("parallel",)),
        name="mla_prep",
    )(z, z, kr, q_norm, kv_norm, wuq_p, wukv, cos_t, sin_t)


RPB_ROWS = 2 * NA_WIN_ROWS - 1
RPB_COLS = 2 * NA_WIN_COLS - 1
NA_Q_ROWS = 4
NA_K_ROWS = NA_Q_ROWS + NA_WIN_ROWS
NA_VARIANTS = 3


def _na_variant_geometry(var, j):
    lo = (0, j, NA_K_ROWS - NA_WIN_ROWS)[var]
    rho0 = (NA_WIN_ROWS - 1) - j - (NA_WIN_ROWS // 2) * var
    return lo, rho0


def _na_bias_body(rpb_ref, o_ref, pair_ref):
    h = pl.program_id(0)
    kw = NA_WIN_COLS
    shape = (GRID_W, LANES)
    q = lax.broadcasted_iota(jnp.int32, shape, 0)
    lane = lax.broadcasted_iota(jnp.int32, shape, 1)
    first = lane < GRID_W
    k = jnp.where(first, lane, lane - GRID_W)
    col_start = jnp.clip(q - kw // 2, 0, GRID_W - kw)
    valid = (k >= col_start) & (k < col_start + kw)
    dc = jnp.clip(k - q, -(kw - 1), kw - 1) + (kw - 1)
    neg = jnp.full(shape, NEG_INF, F32)

    for e in range(RPB_ROWS + 1):
        base_a = (h * RPB_ROWS + max(e - 1, 0)) * RPB_COLS
        base_b = (h * RPB_ROWS + min(e, RPB_ROWS - 1)) * RPB_COLS
        acc = jnp.zeros(shape, F32)
        for c in range(RPB_COLS):
            val = jnp.where(first, rpb_ref[base_a + c], rpb_ref[base_b + c])
            acc = jnp.where(dc == c, val, acc)
        pair_ref[e] = jnp.where(valid, acc * LOG2E, NEG_INF)

    for var in range(NA_VARIANTS):
        for j in range(NA_Q_ROWS):
            lo, rho0 = _na_variant_geometry(var, j)
            for pair in range(NA_K_ROWS // 2):
                i = 2 * pair
                in_a = lo <= i < lo + NA_WIN_ROWS
                in_b = lo <= i + 1 < lo + NA_WIN_ROWS
                if in_a or in_b:
                    tile = pair_ref[rho0 + i + 1]
                    if not in_a:
                        tile = jnp.where(first, neg, tile)
                    if not in_b:
                        tile = jnp.where(first, tile, neg)
                else:
                    tile = neg
                o_ref[var, 0, j * GRID_W:(j + 1) * GRID_W, pair * LANES:(pair + 1) * LANES] = tile


def _na_bias(rpb_flat):
    nq = NA_Q_ROWS * GRID_W
    nk = NA_K_ROWS * GRID_W
    return pl.pallas_call(
        _na_bias_body,
        grid=(NA_HEADS,),
        in_specs=[pl.BlockSpec(memory_space=pltpu.SMEM)],
        out_specs=pl.BlockSpec((NA_VARIANTS, 1, nq, nk), lambda h: (0, h, 0, 0)),
        out_shape=jax.ShapeDtypeStruct((NA_VARIANTS, NA_HEADS, nq, nk), F32),
        scratch_shapes=[pltpu.VMEM((RPB_ROWS + 1, GRID_W, LANES), F32)],
        compiler_params=_cparams(("arbitrary",)),
        name="na_bias",
    )(rpb_flat)


def _na_key_start(g, rows):
    return jnp.clip(g * NA_Q_ROWS - NA_WIN_ROWS // 2, 0, rows - NA_K_ROWS)


def _na_body(q_ref, k_ref, v_ref, b_ref, o_ref, *, rows):
    g = pl.program_id(1)
    start = pl.multiple_of(_na_key_start(g, rows) * GRID_W, GRID_W)
    kind = jnp.where(g == 0, 0, jnp.where(g == rows // NA_Q_ROWS - 1, 2, 1))
    nkeys = NA_K_ROWS * GRID_W
    c = (NA_HEAD_DIM ** -0.5) * LOG2E
    ones = jnp.ones((nkeys, NA_HEAD_DIM), BF16)
    for h in range(NA_HEADS):
        cs = slice(h * NA_HEAD_DIM, (h + 1) * NA_HEAD_DIM)
        q = q_ref[0, :, cs]
        k = k_ref[0, pl.ds(start, nkeys), cs]
        v = v_ref[0, pl.ds(start, nkeys), cs]
        s = _dot_nt(q, k)
        t = s * c + b_ref[kind, h]
        m = jnp.max(t, axis=-1, keepdims=True)
        p = jnp.exp2(t - m).astype(BF16)
        oe = jnp.dot(p, jnp.concatenate([v, ones], axis=1), preferred_element_type=F32)
        o_ref[0, :, cs] = (oe[:, :NA_HEAD_DIM] / oe[:, NA_HEAD_DIM:]).astype(BF16)


def _na_attn(z3, bias, *, q_blk, k_blk, v_blk):
    b, s, _ = z3.shape
    rows = s // GRID_W
    assert rows % NA_Q_ROWS == 0 and rows >= 2 * NA_K_ROWS - NA_WIN_ROWS
    groups = rows // NA_Q_ROWS
    nq = NA_Q_ROWS * GRID_W
    return pl.pallas_call(
        functools.partial(_na_body, rows=rows),
        grid=(b, groups),
        in_specs=[
            pl.BlockSpec((1, nq, NA_WIDTH), lambda bi, g: (bi, g, q_blk)),
            pl.BlockSpec((1, s, NA_WIDTH), lambda bi, g: (bi, 0, k_blk)),
            pl.BlockSpec((1, s, NA_WIDTH), lambda bi, g: (bi, 0, v_blk)),
            pl.BlockSpec(bias.shape, lambda bi, g: (0, 0, 0, 0), pipeline_mode=pl.Buffered(1)),
        ],
        out_specs=pl.BlockSpec((1, nq, NA_WIDTH), lambda bi, g: (bi, g, 0)),
        out_shape=jax.ShapeDtypeStruct((b, s, NA_WIDTH), BF16),
        compiler_params=_cparams(("parallel", "arbitrary")),
        name="na_attn",
    )(z3, z3, z3, bias)


def _mla_body(*refs, n_side, tq, tk):
    q_ref, k_ref, v_ref = refs[:3]
    o_ref = refs[3 + n_side]
    _run_side_jobs(refs[3:3 + n_side], refs[4 + n_side:], None, None)
    seq = q_ref.shape[1]
    c = (MLA_QK_DIM ** -0.5) * LOG2E
    for qi in range(seq // tq):
        q = q_ref[0, qi * tq:(qi + 1) * tq, :]
        m = acc = None
        for kj in range(seq // tk):
            ks = slice(kj * tk, (kj + 1) * tk)
            s = _dot_nt(q, k_ref[0, ks, :]) * c
            mj = jnp.max(s, axis=-1, keepdims=True)
            m_new = mj if m is None else jnp.maximum(m, mj)
            p = jnp.exp2(s - m_new).astype(BF16)
            pv = jnp.dot(p, v_ref[0, ks, :], preferred_element_type=F32)
            acc = pv if acc is None else acc * jnp.exp2(m - m_new) + pv
            m = m_new
        o_ref[0, qi * tq:(qi + 1) * tq, :] = (acc[:, :MLA_V_DIM] / acc[:, MLA_V_DIM:]).astype(BF16)


def _mla_attn(q3, k3, v3, *, casts=(), tq=512, tk=512):
    b, s, _ = q3.shape
    vw = v3.shape[2] // MLA_HEADS
    side = _side_job_specs(casts, None, None, b * MLA_HEADS, lambda bi, h: bi * MLA_HEADS + h)
    outs = pl.pallas_call(
        functools.partial(_mla_body, n_side=len(side[0]), tq=tq, tk=tk),
        grid=(b, MLA_HEADS),
        in_specs=[
            pl.BlockSpec((1, s, MLA_HEAD_PAD), lambda bi, h: (bi, 0, h)),
            pl.BlockSpec((1, s, MLA_HEAD_PAD), lambda bi, h: (bi, 0, h)),
            pl.BlockSpec((1, s, vw), lambda bi, h: (bi, 0, h)),
        ] + side[0],
        out_specs=[pl.BlockSpec((1, s, MLA_V_DIM), lambda bi, h: (bi, 0, h))] + side[1],
        out_shape=[jax.ShapeDtypeStruct((b, s, MLA_WIDTH), BF16)] + side[2],
        compiler_params=_cparams(("arbitrary", "arbitrary")),
        name="mla_attn",
    )(q3, k3, v3, *side[3])
    return outs[0], list(outs[1:])


def _merge_body(na_ref, mla_ref, ga_ref, gb_ref, h_ref, wa_ref, wb_ref, wo_ref, nw_ref,
                o_ref, xn_ref):
    ya = jnp.dot(na_ref[...], wa_ref[...], preferred_element_type=F32)
    yb = jnp.dot(mla_ref[...], wb_ref[...], preferred_element_type=F32)
    merged = _sigmoid(ga_ref[...].astype(F32)) * ya + _sigmoid(gb_ref[...].astype(F32)) * yb
    h = h_ref[...] + jnp.dot(merged.astype(BF16), wo_ref[...], preferred_element_type=F32)
    o_ref[...] = h
    xn_ref[...] = _rms(h, nw_ref[...]).astype(BF16)


def _merge(na_o, mla_o, z, h, wa, wb, wo, next_norm_w, *, ga_blk, gb_blk, tm=256):
    t, d = h.shape
    return pl.pallas_call(
        _merge_body,
        grid=(t // tm,),
        in_specs=[
            pl.BlockSpec((tm, NA_WIDTH), lambda i: (i, 0)),
            pl.BlockSpec((tm, MLA_WIDTH), lambda i: (i, 0)),
            pl.BlockSpec((tm, d), lambda i: (i, ga_blk)),
            pl.BlockSpec((tm, d), lambda i: (i, gb_blk)),
            pl.BlockSpec((tm, d), lambda i: (i, 0)),
            _resident((NA_WIDTH, d)),
            _resident((MLA_WIDTH, d)),
            _resident((d, d)),
            _resident((1, d)),
        ],
        out_specs=[pl.BlockSpec((tm, d), lambda i: (i, 0))] * 2,
        out_shape=[jax.ShapeDtypeStruct((t, d), F32), jax.ShapeDtypeStruct((t, d), BF16)],
        compiler_params=_cparams(("parallel",)),
        name="merge",
    )(na_o, mla_o, z, z, h, wa, wb, wo, next_norm_w)


def _pl_body(h_ref, p_ref, nw_ref, wg_ref, wp_ref, fn_ref, o_ref, *, final):
    h = h_ref[...]
    xn = _rms(h, nw_ref[...]).astype(BF16)
    gate = _sigmoid(jnp.dot(xn, wg_ref[...], preferred_element_type=F32))
    emb = jnp.dot(p_ref[...].astype(BF16), wp_ref[...], preferred_element_type=F32)
    h = h + gate * emb
    o_ref[...] = _rms(h, fn_ref[...]) if final else h


def _pl_embed(h, p, norm_w, w_gate, w_pl, final_norm, *, final, tm=512):
    t, d = h.shape
    pdim = p.shape[1]
    return pl.pallas_call(
        functools.partial(_pl_body, final=final),
        grid=(t // tm,),
        in_specs=[
            pl.BlockSpec((tm, d), lambda i: (i, 0)),
            pl.BlockSpec((tm, pdim), lambda i: (i, 0)),
            _resident((1, d)),
            _resident((d, d)),
            _resident((pdim, d)),
            _resident((1, d)),
        ],
        out_specs=pl.BlockSpec((tm, d), lambda i: (i, 0)),
        out_shape=jax.ShapeDtypeStruct((t, d), F32),
        compiler_params=_cparams(("parallel",)),
        name="pl_final" if final else "pl_embed",
    )(h, p, norm_w, w_gate, w_pl, final_norm)


def _final_norm_body(h_ref, fn_ref, o_ref):
    o_ref[...] = _rms(h_ref[...], fn_ref[...])


def _rope_tables(seq):
    half = MLA_ROPE_DIM // 2
    pos = jnp.arange(seq, dtype=F32)
    inv_freq = 1.0 / (ROPE_THETA ** (jnp.arange(0, MLA_ROPE_DIM, 2, dtype=F32) / MLA_ROPE_DIM))
    ang = pos[:, None] * inv_freq[None, :]
    cos, sin = jnp.cos(ang), jnp.sin(ang)
    zeros = jnp.zeros((seq, LANES - 2 * half), F32)
    return (jnp.concatenate([cos, cos, zeros], axis=-1),
            jnp.concatenate([-sin, sin, zeros], axis=-1))


def kernel(x, p, ffn1_norm, ffn1_w_gate, ffn1_w_up, ffn1_w_down, mix_norm, w_in, q_a_norm, w_uq, kv_a_norm, w_ukv, na_rpb, w_branch_a, w_branch_b, w_out, ffn2_norm, ffn2_w_gate, ffn2_w_up, ffn2_w_down, pl_norm, w_pl, w_pl_gate, final_norm):
    bsz, seq, d = x.shape
    depth = p.shape[0]
    t = bsz * seq
    bf = lambda w: w.astype(BF16)
    row = lambda g: g.reshape(1, -1).astype(F32)

    c_lat = 3 * NA_WIDTH
    c_rope = c_lat + MLA_Q_RANK + MLA_KV_RANK
    cos_t, sin_t = _rope_tables(seq)

    h = x.reshape(t, d)
    for i in range(depth):
        (h, xn_mix), (w_main_t, w_kr_t) = _ffn(
            h, row(ffn1_norm[i]), ffn1_w_gate[i], ffn1_w_up[i], ffn1_w_down[i],
            next_norm_w=row(mix_norm[i]),
            split_src=jnp.swapaxes(w_in[i], 0, 1), split=(c_rope, MLA_ROPE_DIM))
        z, kr = _in_proj(xn_mix, w_main_t, w_kr_t)

        wuq = w_uq[i].reshape(MLA_Q_RANK, MLA_HEADS, MLA_QK_DIM)
        wuq_p = bf(jnp.pad(wuq, ((0, 0), (0, 0), (0, MLA_HEAD_PAD - MLA_QK_DIM)))
                   .reshape(MLA_Q_RANK, MLA_HEADS * MLA_HEAD_PAD))
        q_m, k_m, v_m = _mla_prep(
            z, kr, row(q_a_norm[i]), row(kv_a_norm[i]), wuq_p, bf(w_ukv[i]), cos_t, sin_t,
            seq=seq, ql_blk=c_lat // MLA_Q_RANK, kvl_blk=(c_lat + MLA_Q_RANK) // MLA_KV_RANK)

        bias = _na_bias(na_rpb[i].reshape(-1).astype(F32))
        z3 = z.reshape(bsz, seq, z.shape[1])
        na_o = _na_attn(z3, bias, q_blk=0, k_blk=1, v_blk=2)
        later = [ffn2_w_gate[i], ffn2_w_up[i], ffn2_w_down[i], w_branch_a[i], w_branch_b[i],
                 w_out[i], w_pl_gate[i]]
        mla_o, (w2_gate, w2_up, w2_down, wa, wb, wo, w_plg) = _mla_attn(
            q_m.reshape(bsz, seq, -1), k_m.reshape(bsz, seq, -1), v_m.reshape(bsz, seq, -1),
            casts=later)

        gate_col = c_rope
        h, xn2 = _merge(na_o.reshape(t, NA_WIDTH), mla_o.reshape(t, MLA_WIDTH), z, h, wa, wb, wo,
                        row(ffn2_norm[i]), ga_blk=gate_col // d, gb_blk=gate_col // d + 1)

        (h, _), _ = _ffn(h, row(ffn2_norm[i]), w2_gate, w2_up, w2_down, xn=xn2)

        h = _pl_embed(h, p[i].reshape(t, -1), row(pl_norm[i]), w_plg, bf(w_pl[i]),
                      row(final_norm), final=(i == depth - 1))

    if depth == 0:
        tm = 512
        h = pl.pallas_call(
            _final_norm_body,
            grid=(t // tm,),
            in_specs=[pl.BlockSpec((tm, d), lambda i: (i, 0)),
                      pl.BlockSpec((1, d), lambda i: (0, 0))],
            out_specs=pl.BlockSpec((tm, d), lambda i: (i, 0)),
            out_shape=jax.ShapeDtypeStruct((t, d), F32),
            compiler_params=_cparams(("parallel",)),
            name="final_norm",
        )(h, row(final_norm))
    return h.reshape(bsz, seq, d)
```

```python
import functools
import math

import jax
import jax.numpy as jnp
from jax import lax
from jax.experimental import pallas as pl
from jax.experimental.pallas import tpu as pltpu

F32 = jnp.float32
BF16 = jnp.bfloat16

GRID_W = 64
NA_HEADS = 8
NA_HEAD_DIM = 128
NA_WIN_ROWS = 8
NA_WIN_COLS = 16
NA_WIDTH = NA_HEADS * NA_HEAD_DIM
MLA_HEADS = 8
MLA_Q_RANK = 512
MLA_KV_RANK = 512
MLA_NOPE_DIM = 128
MLA_ROPE_DIM = 64
MLA_V_DIM = 128
MLA_QK_DIM = MLA_NOPE_DIM + MLA_ROPE_DIM
MLA_WIDTH = MLA_HEADS * MLA_V_DIM
ROPE_THETA = 10000.0
NORM_EPS = 1e-6
NEG_INF = -1e30
LOG2E = math.log2(math.e)

LANES = 128
BF16_ROWS = 16
MLA_HEAD_PAD = 2 * LANES
VMEM_LIMIT_BYTES = 56 * 1024 * 1024


def _cparams(sem):
    return pltpu.CompilerParams(dimension_semantics=sem, vmem_limit_bytes=VMEM_LIMIT_BYTES)


def _resident(shape):
    return pl.BlockSpec(shape, lambda *_: (0, 0), pipeline_mode=pl.Buffered(1))


def _rms(x, g):
    ms = jnp.mean(x * x, axis=-1, keepdims=True)
    return x * lax.rsqrt(ms + NORM_EPS) * g


def _sigmoid(x):
    return 1.0 / (1.0 + jnp.exp(-x))


def _dot_nt(a, b):
    return lax.dot_general(a, b, (((1,), (1,)), ((), ())), preferred_element_type=F32)


def _chunk_rows(rows, nsteps):
    rc = BF16_ROWS
    while rows % rc or rows // rc > nsteps:
        rc += BF16_ROWS
    return rc


def _side_job_specs(casts, split_src, split, nsteps, lin):
    def spec(rows, cols):
        rc = _chunk_rows(rows, nsteps)
        last = rows // rc - 1
        return pl.BlockSpec((rc, cols), lambda *g: (jnp.minimum(lin(*g), last), 0))

    in_specs, out_specs, out_shape, operands = [], [], [], []
    for w in casts:
        in_specs.append(spec(*w.shape))
        out_specs.append(spec(*w.shape))
        out_shape.append(jax.ShapeDtypeStruct(w.shape, BF16))
        operands.append(w)
    if split_src is not None:
        rows, cols = split_src.shape
        keep, skip = split
        assert keep % skip == 0 and rows % skip == 0 and rows // skip <= nsteps and skip <= LANES
        last, hole = rows // skip - 1, keep // skip

        def packed_idx(*g):
            c = jnp.minimum(lin(*g), last)
            return (jnp.where(c <= hole, jnp.minimum(c, hole - 1), c - 1), 0)

        in_specs.append(pl.BlockSpec((skip, cols), lambda *g: (jnp.minimum(lin(*g), last), 0)))
        out_specs += [pl.BlockSpec((skip, cols), packed_idx),
                      pl.BlockSpec((LANES, cols), lambda *g: (0, 0))]
        out_shape += [jax.ShapeDtypeStruct((rows - skip, cols), BF16),
                      jax.ShapeDtypeStruct((LANES, cols), BF16)]
        operands.append(split_src)
    return in_specs, out_specs, out_shape, operands


def _run_side_jobs(in_refs, out_refs, split, step):
    n_cast = len(in_refs) - (1 if split else 0)
    for src, dst in zip(in_refs[:n_cast], out_refs[:n_cast]):
        dst[...] = src[...].astype(BF16)
    if split:
        keep, skip = split
        src = in_refs[n_cast]
        packed_ref, moved_ref = out_refs[n_cast:]
        hole = keep // skip

        @pl.when(step != hole)
        def _():
            packed_ref[...] = src[...].astype(BF16)

        @pl.when(step == hole)
        def _():
            moved_ref[:skip, :] = src[...].astype(BF16)
            if skip < LANES:
                moved_ref[skip:, :] = jnp.zeros((LANES - skip, src.shape[1]), BF16)


FFN_UP_ROWS = 1024


def _ffn_up_body(*refs, n_side, split, prenormed):
    x_ref, nw_ref, wg_ref, wu_ref = refs[:4]
    side_in = refs[4:4 + n_side]
    a_ref = refs[4 + n_side]
    j, k = pl.program_id(1), pl.program_id(2)
    if prenormed:
        side_out = refs[5 + n_side:]
        xn_ref = x_ref
    else:
        side_out = refs[5 + n_side:-1]
        xn_ref = refs[-1].at[k]

        @pl.when(j == 0)
        def _():
            xn_ref[...] = _rms(x_ref[...], nw_ref[...]).astype(BF16)

    wg = wg_ref[...].astype(BF16)
    wu = wu_ref[...].astype(BF16)
    rows_per = min(xn_ref.shape[0], FFN_UP_ROWS)
    for r in range(xn_ref.shape[0] // rows_per):
        rows = slice(r * rows_per, (r + 1) * rows_per)
        xn = xn_ref[rows, :]
        g = jnp.dot(xn, wg, preferred_element_type=F32)
        u = jnp.dot(xn, wu, preferred_element_type=F32)
        a_ref[rows, :] = (g * _sigmoid(g) * u * 0.5).astype(BF16)
    step = (pl.program_id(0) * pl.num_programs(1) + j) * pl.num_programs(2) + k
    _run_side_jobs(side_in, side_out, split, step)


def _ffn_down_body(a_ref, wd_ref, x_ref, nw_ref, o_ref, *xn_ref):
    h = x_ref[...] + jnp.dot(a_ref[...], wd_ref[...], preferred_element_type=F32)
    o_ref[...] = h
    if xn_ref:
        xn_ref[0][...] = _rms(h, nw_ref[...]).astype(BF16)


def _ffn(x, norm_w, wg, wu, wd, *, xn=None, next_norm_w=None, casts=(), split_src=None, split=None,
         tm=1024, tf=512, tm_down=256):
    t, d = x.shape
    f = wg.shape[1]
    nj = f // tf
    casts = list(casts)
    if wd.dtype != BF16:
        casts.append(wd)
    prenormed = xn is not None
    tm_up = 2 * tm if prenormed else tm
    ntiles = t // tm_up
    pair = 2 if not prenormed and ntiles % 2 == 0 else 1
    tile = lambda p, k: p * pair + k
    x_idx = (lambda p, j, k: (tile(p, k), 0)) if prenormed else (
        lambda p, j, k: (jnp.where(j == 0, tile(p, k), tile(p, pair - 1)), 0))
    side = _side_job_specs(casts, split_src, split, ntiles * nj,
                           lambda p, j, k: (p * nj + j) * pair + k)
    outs = pl.pallas_call(
        functools.partial(_ffn_up_body, n_side=len(side[0]), prenormed=prenormed,
                          split=split if split_src is not None else None),
        grid=(ntiles // pair, nj, pair),
        in_specs=[
            pl.BlockSpec((tm_up, d), x_idx),
            pl.BlockSpec((1, d), lambda p, j, k: (0, 0)),
            pl.BlockSpec((d, tf), lambda p, j, k: (0, j)),
            pl.BlockSpec((d, tf), lambda p, j, k: (0, j)),
        ] + side[0],
        out_specs=[pl.BlockSpec((tm_up, tf), lambda p, j, k: (tile(p, k), j))] + side[1],
        out_shape=[jax.ShapeDtypeStruct((t, f), BF16)] + side[2],
        scratch_shapes=[] if prenormed else [pltpu.VMEM((pair, tm, d), BF16)],
        compiler_params=_cparams(("arbitrary", "arbitrary", "arbitrary")),
        name="ffn_up",
    )(xn if prenormed else x, norm_w, wg, wu, *side[3])
    a = outs[0]
    converted = list(outs[1:])
    if wd.dtype != BF16:
        wd = converted.pop(len(casts) - 1)
    emit_norm = next_norm_w is not None
    row_spec = pl.BlockSpec((tm_down, d), lambda i: (i, 0))
    outs = pl.pallas_call(
        _ffn_down_body,
        grid=(t // tm_down,),
        in_specs=[
            pl.BlockSpec((tm_down, f), lambda i: (i, 0)),
            _resident((f, d)),
            row_spec,
            _resident((1, d)),
        ],
        out_specs=[row_spec] * (2 if emit_norm else 1),
        out_shape=[jax.ShapeDtypeStruct((t, d), F32)]
        + ([jax.ShapeDtypeStruct((t, d), BF16)] if emit_norm else []),
        compiler_params=_cparams(("parallel",)),
        name="ffn_down",
    )(a, wd, x, next_norm_w if emit_norm else norm_w)
    return (outs[0], outs[1] if emit_norm else None), converted


IN_PROJ_ROWS = 1024


def _in_proj_body(xn_ref, w_ref, wkr_ref, z_ref, kr_ref):
    @pl.when(pl.program_id(1) == 0)
    def _():
        kr_ref[...] = _dot_nt(xn_ref[...], wkr_ref[...])

    rows_per = min(xn_ref.shape[0], IN_PROJ_ROWS)
    for r in range(xn_ref.shape[0] // rows_per):
        rows = slice(r * rows_per, (r + 1) * rows_per)
        z_ref[rows, :] = _dot_nt(xn_ref[rows, :], w_ref[...]).astype(BF16)


def _in_proj(xn, w_main_t, w_kr_t, *, tm=2048, tn=1024):
    t, d = xn.shape
    n = w_main_t.shape[0]
    return pl.pallas_call(
        _in_proj_body,
        grid=(t // tm, n // tn),
        in_specs=[
            pl.BlockSpec((tm, d), lambda i, j: (i, 0)),
            pl.BlockSpec((tn, d), lambda i, j: (j, 0)),
            pl.BlockSpec((LANES, d), lambda i, j: (0, 0)),
        ],
        out_specs=[
            pl.BlockSpec((tm, tn), lambda i, j: (i, j)),
            pl.BlockSpec((tm, LANES), lambda i, j: (i, 0)),
        ],
        out_shape=[
            jax.ShapeDtypeStruct((t, n), BF16),
            jax.ShapeDtypeStruct((t, LANES), F32),
        ],
        compiler_params=_cparams(("parallel", "arbitrary")),
        name="in_proj",
    )(xn, w_main_t, w_kr_t)


def _rope(r, cos_t, sin_t):
    lane = lax.broadcasted_iota(jnp.int32, r.shape, 1)
    half = MLA_ROPE_DIM // 2
    swapped = jnp.where(lane < half, pltpu.roll(r, LANES - half, 1), pltpu.roll(r, half, 1))
    return r * cos_t + swapped * sin_t


def _mla_prep_body(ql_ref, kvl_ref, kr_ref, qn_ref, kvn_ref, wuq_ref, wukv_ref,
                   cos_ref, sin_ref, q_out, k_out, v_out):
    cq = _rms(ql_ref[...].astype(F32), qn_ref[...]).astype(BF16)
    ckv = _rms(kvl_ref[...].astype(F32), kvn_ref[...]).astype(BF16)
    cos_t = cos_ref[...]
    sin_t = sin_ref[...]
    k_rope = _rope(kr_ref[...], cos_t, sin_t).astype(BF16)
    for h in range(MLA_HEADS):
        lo = h * MLA_HEAD_PAD
        mid = lo + LANES
        hi = lo + MLA_HEAD_PAD
        q = jnp.dot(cq, wuq_ref[:, lo:hi], preferred_element_type=F32)
        kv = jnp.dot(ckv, wukv_ref[:, lo:hi], preferred_element_type=F32)
        q_out[:, lo:mid] = q[:, :LANES].astype(BF16)
        q_out[:, mid:hi] = _rope(q[:, LANES:], cos_t, sin_t).astype(BF16)
        k_out[:, lo:mid] = kv[:, :LANES].astype(BF16)
        k_out[:, mid:hi] = k_rope
        v_out[:, lo:mid] = kv[:, LANES:].astype(BF16)
        v_out[:, mid:hi] = jnp.ones((kv.shape[0], MLA_HEAD_PAD - MLA_V_DIM), BF16)


def _mla_prep(z, kr, q_norm, kv_norm, wuq_p, wukv, cos_t, sin_t, *, seq, ql_blk, kvl_blk, tm=512):
    t = z.shape[0]
    nseq = seq // tm
    qw = MLA_HEADS * MLA_HEAD_PAD
    return pl.pallas_call(
        _mla_prep_body,
        grid=(t // tm,),
        in_specs=[
            pl.BlockSpec((tm, MLA_Q_RANK), lambda i: (i, ql_blk)),
            pl.BlockSpec((tm, MLA_KV_RANK), lambda i: (i, kvl_blk)),
            pl.BlockSpec((tm, LANES), lambda i: (i, 0)),
            pl.BlockSpec((1, MLA_Q_RANK), lambda i: (0, 0)),
            pl.BlockSpec((1, MLA_KV_RANK), lambda i: (0, 0)),
            pl.BlockSpec((MLA_Q_RANK, qw), lambda i: (0, 0)),
            pl.BlockSpec((MLA_KV_RANK, qw), lambda i: (0, 0)),
            pl.BlockSpec((tm, LANES), lambda i: (i % nseq, 0)),
            pl.BlockSpec((tm, LANES), lambda i: (i % nseq, 0)),
        ],
        out_specs=[
            pl.BlockSpec((tm, qw), lambda i: (i, 0)),
            pl.BlockSpec((tm, qw), lambda i: (i, 0)),
            pl.BlockSpec((tm, qw), lambda i: (i, 0)),
        ],
        out_shape=[
            jax.ShapeDtypeStruct((t, qw), BF16),
            jax.ShapeDtypeStruct((t, qw), BF16),
            jax.ShapeDtypeStruct((t, qw), BF16),
        ],
        compiler_params=_cparams(("parallel",)),
        name="mla_prep",
    )(z, z, kr, q_norm, kv_norm, wuq_p, wukv, cos_t, sin_t)


RPB_ROWS = 2 * NA_WIN_ROWS - 1
RPB_COLS = 2 * NA_WIN_COLS - 1
NA_Q_ROWS = 4
NA_K_ROWS = NA_Q_ROWS + NA_WIN_ROWS
NA_VARIANTS = 3


def _na_variant_geometry(var, j):
    lo = (0, j, NA_K_ROWS - NA_WIN_ROWS)[var]
    rho0 = (NA_WIN_ROWS - 1) - j - (NA_WIN_ROWS // 2) * var
    return lo, rho0


def _na_bias_body(rpb_ref, o_ref, pair_ref):
    h = pl.program_id(0)
    kw = NA_WIN_COLS
    shape = (GRID_W, LANES)
    q = lax.broadcasted_iota(jnp.int32, shape, 0)
    lane = lax.broadcasted_iota(jnp.int32, shape, 1)
    first = lane < GRID_W
    k = jnp.where(first, lane, lane - GRID_W)
    col_start = jnp.clip(q - kw // 2, 0, GRID_W - kw)
    valid = (k >= col_start) & (k < col_start + kw)
    dc = jnp.clip(k - q, -(kw - 1), kw - 1) + (kw - 1)
    neg = jnp.full(shape, NEG_INF, F32)

    for e in range(RPB_ROWS + 1):
        base_a = (h * RPB_ROWS + max(e - 1, 0)) * RPB_COLS
        base_b = (h * RPB_ROWS + min(e, RPB_ROWS - 1)) * RPB_COLS
        acc = jnp.zeros(shape, F32)
        for c in range(RPB_COLS):
            val = jnp.where(first, rpb_ref[base_a + c], rpb_ref[base_b + c])
            acc = jnp.where(dc == c, val, acc)
        pair_ref[e] = jnp.where(valid, acc * LOG2E, NEG_INF)

    for var in range(NA_VARIANTS):
        for j in range(NA_Q_ROWS):
            lo, rho0 = _na_variant_geometry(var, j)
            for pair in range(NA_K_ROWS // 2):
                i = 2 * pair
                in_a = lo <= i < lo + NA_WIN_ROWS
                in_b = lo <= i + 1 < lo + NA_WIN_ROWS
                if in_a or in_b:
                    tile = pair_ref[rho0 + i + 1]
                    if not in_a:
                        tile = jnp.where(first, neg, tile)
                    if not in_b:
                        tile = jnp.where(first, tile, neg)
                else:
                    tile = neg
                o_ref[var, 0, j * GRID_W:(j + 1) * GRID_W, pair * LANES:(pair + 1) * LANES] = tile


def _na_bias(rpb_flat):
    nq = NA_Q_ROWS * GRID_W
    nk = NA_K_ROWS * GRID_W
    return pl.pallas_call(
        _na_bias_body,
        grid=(NA_HEADS,),
        in_specs=[pl.BlockSpec(memory_space=pltpu.SMEM)],
        out_specs=pl.BlockSpec((NA_VARIANTS, 1, nq, nk), lambda h: (0, h, 0, 0)),
        out_shape=jax.ShapeDtypeStruct((NA_VARIANTS, NA_HEADS, nq, nk), F32),
        scratch_shapes=[pltpu.VMEM((RPB_ROWS + 1, GRID_W, LANES), F32)],
        compiler_params=_cparams(("arbitrary",)),
        name="na_bias",
    )(rpb_flat)


def _na_key_start(g, rows):
    return jnp.clip(g * NA_Q_ROWS - NA_WIN_ROWS // 2, 0, rows - NA_K_ROWS)


def _na_kv_copy(z_hbm, kv_buf, sem, batch, which, col_blk):
    slot = batch % 2
    return pltpu.make_async_copy(z_hbm.at[batch, :, pl.ds(col_blk * NA_WIDTH, NA_WIDTH)],
                                 kv_buf.at[slot, which], sem.at[slot, which])


def _na_body(q_ref, z_hbm, b_ref, o_ref, kv_buf, sem, *, rows, nbatch, k_blk, v_blk):
    bi, g = pl.program_id(0), pl.program_id(1)

    def copies(batch):
        return (_na_kv_copy(z_hbm, kv_buf, sem, batch, 0, k_blk),
                _na_kv_copy(z_hbm, kv_buf, sem, batch, 1, v_blk))

    @pl.when((g == 0) & (bi == 0))
    def _():
        for cp in copies(0):
            cp.start()

    @pl.when((g == 0) & (bi + 1 < nbatch))
    def _():
        for cp in copies(bi + 1):
            cp.start()

    @pl.when(g == 0)
    def _():
        for cp in copies(bi):
            cp.wait()

    k_ref = kv_buf.at[bi % 2, 0]
    v_ref = kv_buf.at[bi % 2, 1]
    start = pl.multiple_of(_na_key_start(g, rows) * GRID_W, GRID_W)
    kind = jnp.where(g == 0, 0, jnp.where(g == rows // NA_Q_ROWS - 1, 2, 1))
    nkeys = NA_K_ROWS * GRID_W
    c = (NA_HEAD_DIM ** -0.5) * LOG2E
    ones = jnp.ones((nkeys, NA_HEAD_DIM), BF16)
    for h in range(NA_HEADS):
        cs = slice(h * NA_HEAD_DIM, (h + 1) * NA_HEAD_DIM)
        q = q_ref[0, :, cs]
        k = k_ref[pl.ds(start, nkeys), cs]
        v = v_ref[pl.ds(start, nkeys), cs]
        s = _dot_nt(q, k)
        t = s * c + b_ref[kind, h]
        m = jnp.max(t, axis=-1, keepdims=True)
        p = jnp.exp2(t - m).astype(BF16)
        oe = jnp.dot(p, jnp.concatenate([v, ones], axis=1), preferred_element_type=F32)
        o_ref[0, :, cs] = (oe[:, :NA_HEAD_DIM] / oe[:, NA_HEAD_DIM:]).astype(BF16)


def _na_attn(z3, bias, *, q_blk, k_blk, v_blk):
    b, s, _ = z3.shape
    rows = s // GRID_W
    assert rows % NA_Q_ROWS == 0 and rows >= 2 * NA_K_ROWS - NA_WIN_ROWS
    groups = rows // NA_Q_ROWS
    nq = NA_Q_ROWS * GRID_W
    return pl.pallas_call(
        functools.partial(_na_body, rows=rows, nbatch=b, k_blk=k_blk, v_blk=v_blk),
        grid=(b, groups),
        in_specs=[
            pl.BlockSpec((1, nq, NA_WIDTH), lambda bi, g: (bi, g, q_blk)),
            pl.BlockSpec(memory_space=pl.ANY),
            pl.BlockSpec(bias.shape, lambda bi, g: (0, 0, 0, 0), pipeline_mode=pl.Buffered(1)),
        ],
        out_specs=pl.BlockSpec((1, nq, NA_WIDTH), lambda bi, g: (bi, g, 0)),
        out_shape=jax.ShapeDtypeStruct((b, s, NA_WIDTH), BF16),
        scratch_shapes=[pltpu.VMEM((2, 2, s, NA_WIDTH), BF16), pltpu.SemaphoreType.DMA((2, 2))],
        compiler_params=_cparams(("arbitrary", "arbitrary")),
        name="na_attn",
    )(z3, z3, bias)


def _mla_body(*refs, n_side, tq, tk):
    q_ref, k_ref, v_ref = refs[:3]
    o_ref = refs[3 + n_side]
    _run_side_jobs(refs[3:3 + n_side], refs[4 + n_side:], None, None)
    seq = q_ref.shape[1]
    c = (MLA_QK_DIM ** -0.5) * LOG2E
    for qi in range(seq // tq):
        q = q_ref[0, qi * tq:(qi + 1) * tq, :]
        m = acc = None
        for kj in range(seq // tk):
            ks = slice(kj * tk, (kj + 1) * tk)
            s = _dot_nt(q, k_ref[0, ks, :]) * c
            mj = jnp.max(s, axis=-1, keepdims=True)
            m_new = mj if m is None else jnp.maximum(m, mj)
            p = jnp.exp2(s - m_new).astype(BF16)
            pv = jnp.dot(p, v_ref[0, ks, :], preferred_element_type=F32)
            acc = pv if acc is None else acc * jnp.exp2(m - m_new) + pv
            m = m_new
        o_ref[0, qi * tq:(qi + 1) * tq, :] = (acc[:, :MLA_V_DIM] / acc[:, MLA_V_DIM:]).astype(BF16)


def _mla_attn(q3, k3, v3, *, casts=(), tq=512, tk=512):
    b, s, _ = q3.shape
    vw = v3.shape[2] // MLA_HEADS
    side = _side_job_specs(casts, None, None, b * MLA_HEADS, lambda bi, h: bi * MLA_HEADS + h)
    outs = pl.pallas_call(
        functools.partial(_mla_body, n_side=len(side[0]), tq=tq, tk=tk),
        grid=(b, MLA_HEADS),
        in_specs=[
            pl.BlockSpec((1, s, MLA_HEAD_PAD), lambda bi, h: (bi, 0, h)),
            pl.BlockSpec((1, s, MLA_HEAD_PAD), lambda bi, h: (bi, 0, h)),
            pl.BlockSpec((1, s, vw), lambda bi, h: (bi, 0, h)),
        ] + side[0],
        out_specs=[pl.BlockSpec((1, s, MLA_V_DIM), lambda bi, h: (bi, 0, h))] + side[1],
        out_shape=[jax.ShapeDtypeStruct((b, s, MLA_WIDTH), BF16)] + side[2],
        compiler_params=_cparams(("arbitrary", "arbitrary")),
        name="mla_attn",
    )(q3, k3, v3, *side[3])
    return outs[0], list(outs[1:])


def _merge_body(na_ref, mla_ref, ga_ref, gb_ref, h_ref, wa_ref, wb_ref, wo_ref, nw_ref,
                o_ref, xn_ref):
    ya = jnp.dot(na_ref[...], wa_ref[...], preferred_element_type=F32)
    yb = jnp.dot(mla_ref[...], wb_ref[...], preferred_element_type=F32)
    merged = _sigmoid(ga_ref[...].astype(F32)) * ya + _sigmoid(gb_ref[...].astype(F32)) * yb
    h = h_ref[...] + jnp.dot(merged.astype(BF16), wo_ref[...], preferred_element_type=F32)
    o_ref[...] = h
    xn_ref[...] = _rms(h, nw_ref[...]).astype(BF16)


def _merge(na_o, mla_o, z, h, wa, wb, wo, next_norm_w, *, ga_blk, gb_blk, tm=256):
    t, d = h.shape
    return pl.pallas_call(
        _merge_body,
        grid=(t // tm,),
        in_specs=[
            pl.BlockSpec((tm, NA_WIDTH), lambda i: (i, 0)),
            pl.BlockSpec((tm, MLA_WIDTH), lambda i: (i, 0)),
            pl.BlockSpec((tm, d), lambda i: (i, ga_blk)),
            pl.BlockSpec((tm, d), lambda i: (i, gb_blk)),
            pl.BlockSpec((tm, d), lambda i: (i, 0)),
            _resident((NA_WIDTH, d)),
            _resident((MLA_WIDTH, d)),
            _resident((d, d)),
            _resident((1, d)),
        ],
        out_specs=[pl.BlockSpec((tm, d), lambda i: (i, 0))] * 2,
        out_shape=[jax.ShapeDtypeStruct((t, d), F32), jax.ShapeDtypeStruct((t, d), BF16)],
        compiler_params=_cparams(("parallel",)),
        name="merge",
    )(na_o, mla_o, z, z, h, wa, wb, wo, next_norm_w)


def _pl_body(h_ref, p_ref, nw_ref, wg_ref, wp_ref, fn_ref, o_ref, *, final):
    h = h_ref[...]
    xn = _rms(h, nw_ref[...]).astype(BF16)
    gate = _sigmoid(jnp.dot(xn, wg_ref[...], preferred_element_type=F32))
    emb = jnp.dot(p_ref[...].astype(BF16), wp_ref[...], preferred_element_type=F32)
    h = h + gate * emb
    o_ref[...] = _rms(h, fn_ref[...]) if final else h


def _pl_embed(h, p, norm_w, w_gate, w_pl, final_norm, *, final, tm=512):
    t, d = h.shape
    pdim = p.shape[1]
    return pl.pallas_call(
        functools.partial(_pl_body, final=final),
        grid=(t // tm,),
        in_specs=[
            pl.BlockSpec((tm, d), lambda i: (i, 0)),
            pl.BlockSpec((tm, pdim), lambda i: (i, 0)),
            _resident((1, d)),
            _resident((d, d)),
            _resident((pdim, d)),
            _resident((1, d)),
        ],
        out_specs=pl.BlockSpec((tm, d), lambda i: (i, 0)),
        out_shape=jax.ShapeDtypeStruct((t, d), F32),
        compiler_params=_cparams(("parallel",)),
        name="pl_final" if final else "pl_embed",
    )(h, p, norm_w, w_gate, w_pl, final_norm)


def _final_norm_body(h_ref, fn_ref, o_ref):
    o_ref[...] = _rms(h_ref[...], fn_ref[...])


def _rope_tables(seq):
    half = MLA_ROPE_DIM // 2
    pos = jnp.arange(seq, dtype=F32)
    inv_freq = 1.0 / (ROPE_THETA ** (jnp.arange(0, MLA_ROPE_DIM, 2, dtype=F32) / MLA_ROPE_DIM))
    ang = pos[:, None] * inv_freq[None, :]
    cos, sin = jnp.cos(ang), jnp.sin(ang)
    zeros = jnp.zeros((seq, LANES - 2 * half), F32)
    return (jnp.concatenate([cos, cos, zeros], axis=-1),
            jnp.concatenate([-sin, sin, zeros], axis=-1))


def kernel(x, p, ffn1_norm, ffn1_w_gate, ffn1_w_up, ffn1_w_down, mix_norm, w_in, q_a_norm, w_uq, kv_a_norm, w_ukv, na_rpb, w_branch_a, w_branch_b, w_out, ffn2_norm, ffn2_w_gate, ffn2_w_up, ffn2_w_down, pl_norm, w_pl, w_pl_gate, final_norm):
    bsz, seq, d = x.shape
    depth = p.shape[0]
    t = bsz * seq
    bf = lambda w: w.astype(BF16)
    row = lambda g: g.reshape(1, -1).astype(F32)

    c_lat = 3 * NA_WIDTH
    c_rope = c_lat + MLA_Q_RANK + MLA_KV_RANK
    cos_t, sin_t = _rope_tables(seq)

    h = x.reshape(t, d)
    for i in range(depth):
        (h, xn_mix), (w_main_t, w_kr_t) = _ffn(
            h, row(ffn1_norm[i]), ffn1_w_gate[i], ffn1_w_up[i], ffn1_w_down[i],
            next_norm_w=row(mix_norm[i]),
            split_src=jnp.swapaxes(w_in[i], 0, 1), split=(c_rope, MLA_ROPE_DIM))
        z, kr = _in_proj(xn_mix, w_main_t, w_kr_t)

        wuq = w_uq[i].reshape(MLA_Q_RANK, MLA_HEADS, MLA_QK_DIM)
        wuq_p = bf(jnp.pad(wuq, ((0, 0), (0, 0), (0, MLA_HEAD_PAD - MLA_QK_DIM)))
                   .reshape(MLA_Q_RANK, MLA_HEADS * MLA_HEAD_PAD))
        q_m, k_m, v_m = _mla_prep(
            z, kr, row(q_a_norm[i]), row(kv_a_norm[i]), wuq_p, bf(w_ukv[i]), cos_t, sin_t,
            seq=seq, ql_blk=c_lat // MLA_Q_RANK, kvl_blk=(c_lat + MLA_Q_RANK) // MLA_KV_RANK)

        bias = _na_bias(na_rpb[i].reshape(-1).astype(F32))
        z3 = z.reshape(bsz, seq, z.shape[1])
        na_o = _na_attn(z3, bias, q_blk=0, k_blk=1, v_blk=2)
        later = [ffn2_w_gate[i], ffn2_w_up[i], ffn2_w_down[i], w_branch_a[i], w_branch_b[i],
                 w_out[i], w_pl_gate[i]]
        mla_o, (w2_gate, w2_up, w2_down, wa, wb, wo, w_plg) = _mla_attn(
            q_m.reshape(bsz, seq, -1), k_m.reshape(bsz, seq, -1), v_m.reshape(bsz, seq, -1),
            casts=later)

        gate_col = c_rope
        h, xn2 = _merge(na_o.reshape(t, NA_WIDTH), mla_o.reshape(t, MLA_WIDTH), z, h, wa, wb, wo,
                        row(ffn2_norm[i]), ga_blk=gate_col // d, gb_blk=gate_col // d + 1)

        (h, _), _ = _ffn(h, row(ffn2_norm[i]), w2_gate, w2_up, w2_down, xn=xn2)

        h = _pl_embed(h, p[i].reshape(t, -1), row(pl_norm[i]), w_plg, bf(w_pl[i]),
                      row(final_norm), final=(i == depth - 1))

    if depth == 0:
        tm = 512
        h = pl.pallas_call(
            _final_norm_body,
            grid=(t // tm,),
            in_specs=[pl.BlockSpec((tm, d), lambda i: (i, 0)),
                      pl.BlockSpec((1, d), lambda i: (0, 0))],
            out_specs=pl.BlockSpec((tm, d), lambda i: (i, 0)),
            out_shape=jax.ShapeDtypeStruct((t, d), F32),
            compiler_params=_cparams(("parallel",)),
            name="final_norm",
        )(h, row(final_norm))
    return h.reshape(bsz, seq, d)
```

```python
import functools
import math

import jax
import jax.numpy as jnp
from jax import lax
from jax.experimental import pallas as pl
from jax.experimental.pallas import tpu as pltpu

F32 = jnp.float32
BF16 = jnp.bfloat16

GRID_W = 64
NA_HEADS = 8
NA_HEAD_DIM = 128
NA_WIN_ROWS = 8
NA_WIN_COLS = 16
NA_WIDTH = NA_HEADS * NA_HEAD_DIM
MLA_HEADS = 8
MLA_Q_RANK = 512
MLA_KV_RANK = 512
MLA_NOPE_DIM = 128
MLA_ROPE_DIM = 64
MLA_V_DIM = 128
MLA_QK_DIM = MLA_NOPE_DIM + MLA_ROPE_DIM
MLA_WIDTH = MLA_HEADS * MLA_V_DIM
ROPE_THETA = 10000.0
NORM_EPS = 1e-6
NEG_INF = -1e30
LOG2E = math.log2(math.e)

LANES = 128
BF16_ROWS = 16
MLA_HEAD_PAD = 2 * LANES
VMEM_LIMIT_BYTES = 56 * 1024 * 1024


def _cparams(sem):
    return pltpu.CompilerParams(dimension_semantics=sem, vmem_limit_bytes=VMEM_LIMIT_BYTES)


def _resident(shape):
    return pl.BlockSpec(shape, lambda *_: (0, 0), pipeline_mode=pl.Buffered(1))


def _rms(x, g):
    ms = jnp.mean(x * x, axis=-1, keepdims=True)
    return x * lax.rsqrt(ms + NORM_EPS) * g


def _sigmoid(x):
    return 1.0 / (1.0 + jnp.exp(-x))


def _dot_nt(a, b):
    return lax.dot_general(a, b, (((1,), (1,)), ((), ())), preferred_element_type=F32)


def _chunk_rows(rows, nsteps):
    rc = BF16_ROWS
    while rows % rc or rows // rc > nsteps:
        rc += BF16_ROWS
    return rc


def _side_job_specs(casts, split_src, split, nsteps, lin):
    def spec(rows, cols):
        rc = _chunk_rows(rows, nsteps)
        last = rows // rc - 1
        return pl.BlockSpec((rc, cols), lambda *g: (jnp.minimum(lin(*g), last), 0))

    in_specs, out_specs, out_shape, operands = [], [], [], []
    for w in casts:
        in_specs.append(spec(*w.shape))
        out_specs.append(spec(*w.shape))
        out_shape.append(jax.ShapeDtypeStruct(w.shape, BF16))
        operands.append(w)
    if split_src is not None:
        rows, cols = split_src.shape
        keep, skip = split
        assert keep % skip == 0 and rows % skip == 0 and rows // skip <= nsteps and skip <= LANES
        last, hole = rows // skip - 1, keep // skip

        def packed_idx(*g):
            c = jnp.minimum(lin(*g), last)
            return (jnp.where(c <= hole, jnp.minimum(c, hole - 1), c - 1), 0)

        in_specs.append(pl.BlockSpec((skip, cols), lambda *g: (jnp.minimum(lin(*g), last), 0)))
        out_specs += [pl.BlockSpec((skip, cols), packed_idx),
                      pl.BlockSpec((LANES, cols), lambda *g: (0, 0))]
        out_shape += [jax.ShapeDtypeStruct((rows - skip, cols), BF16),
                      jax.ShapeDtypeStruct((LANES, cols), BF16)]
        operands.append(split_src)
    return in_specs, out_specs, out_shape, operands


def _run_side_jobs(in_refs, out_refs, split, step):
    n_cast = len(in_refs) - (1 if split else 0)
    for src, dst in zip(in_refs[:n_cast], out_refs[:n_cast]):
        dst[...] = src[...].astype(BF16)
    if split:
        keep, skip = split
        src = in_refs[n_cast]
        packed_ref, moved_ref = out_refs[n_cast:]
        hole = keep // skip

        @pl.when(step != hole)
        def _():
            packed_ref[...] = src[...].astype(BF16)

        @pl.when(step == hole)
        def _():
            moved_ref[:skip, :] = src[...].astype(BF16)
            if skip < LANES:
                moved_ref[skip:, :] = jnp.zeros((LANES - skip, src.shape[1]), BF16)


FFN_UP_ROWS = 1024


def _ffn_up_body(*refs, n_side, split, prenormed):
    x_ref, nw_ref, wg_ref, wu_ref = refs[:4]
    side_in = refs[4:4 + n_side]
    a_ref = refs[4 + n_side]
    j, k = pl.program_id(1), pl.program_id(2)
    if prenormed:
        side_out = refs[5 + n_side:]
        xn_ref = x_ref
    else:
        side_out = refs[5 + n_side:-1]
        xn_ref = refs[-1].at[k]

        @pl.when(j == 0)
        def _():
            xn_ref[...] = _rms(x_ref[...], nw_ref[...]).astype(BF16)

    wg = wg_ref[...].astype(BF16)
    wu = wu_ref[...].astype(BF16)
    rows_per = min(xn_ref.shape[0], FFN_UP_ROWS)
    for r in range(xn_ref.shape[0] // rows_per):
        rows = slice(r * rows_per, (r + 1) * rows_per)
        xn = xn_ref[rows, :]
        g = jnp.dot(xn, wg, preferred_element_type=F32)
        u = jnp.dot(xn, wu, preferred_element_type=F32)
        a_ref[rows, :] = (g * _sigmoid(g) * u * 0.5).astype(BF16)
    step = (pl.program_id(0) * pl.num_programs(1) + j) * pl.num_programs(2) + k
    _run_side_jobs(side_in, side_out, split, step)


def _ffn_down_body(a_ref, wd_ref, x_ref, nw_ref, o_ref, *xn_ref):
    h = x_ref[...] + jnp.dot(a_ref[...], wd_ref[...], preferred_element_type=F32)
    o_ref[...] = h
    if xn_ref:
        xn_ref[0][...] = _rms(h, nw_ref[...]).astype(BF16)


def _ffn(x, norm_w, wg, wu, wd, *, xn=None, next_norm_w=None, casts=(), split_src=None, split=None,
         tm=1024, tf=512, tm_down=256):
    t, d = x.shape
    f = wg.shape[1]
    nj = f // tf
    casts = list(casts)
    if wd.dtype != BF16:
        casts.append(wd)
    prenormed = xn is not None
    tm_up = 2 * tm if prenormed else tm
    ntiles = t // tm_up
    pair = 2 if not prenormed and ntiles % 2 == 0 else 1
    tile = lambda p, k: p * pair + k
    x_idx = (lambda p, j, k: (tile(p, k), 0)) if prenormed else (
        lambda p, j, k: (jnp.where(j == 0, tile(p, k), tile(p, pair - 1)), 0))
    side = _side_job_specs(casts, split_src, split, ntiles * nj,
                           lambda p, j, k: (p * nj + j) * pair + k)
    outs = pl.pallas_call(
        functools.partial(_ffn_up_body, n_side=len(side[0]), prenormed=prenormed,
                          split=split if split_src is not None else None),
        grid=(ntiles // pair, nj, pair),
        in_specs=[
            pl.BlockSpec((tm_up, d), x_idx),
            pl.BlockSpec((1, d), lambda p, j, k: (0, 0)),
            pl.BlockSpec((d, tf), lambda p, j, k: (0, j)),
            pl.BlockSpec((d, tf), lambda p, j, k: (0, j)),
        ] + side[0],
        out_specs=[pl.BlockSpec((tm_up, tf), lambda p, j, k: (tile(p, k), j))] + side[1],
        out_shape=[jax.ShapeDtypeStruct((t, f), BF16)] + side[2],
        scratch_shapes=[] if prenormed else [pltpu.VMEM((pair, tm, d), BF16)],
        compiler_params=_cparams(("arbitrary", "arbitrary", "arbitrary")),
        name="ffn_up",
    )(xn if prenormed else x, norm_w, wg, wu, *side[3])
    a = outs[0]
    converted = list(outs[1:])
    if wd.dtype != BF16:
        wd = converted.pop(len(casts) - 1)
    emit_norm = next_norm_w is not None
    row_spec = pl.BlockSpec((tm_down, d), lambda i: (i, 0))
    outs = pl.pallas_call(
        _ffn_down_body,
        grid=(t // tm_down,),
        in_specs=[
            pl.BlockSpec((tm_down, f), lambda i: (i, 0)),
            _resident((f, d)),
            row_spec,
            _resident((1, d)),
        ],
        out_specs=[row_spec] * (2 if emit_norm else 1),
        out_shape=[jax.ShapeDtypeStruct((t, d), F32)]
        + ([jax.ShapeDtypeStruct((t, d), BF16)] if emit_norm else []),
        compiler_params=_cparams(("parallel",)),
        name="ffn_down",
    )(a, wd, x, next_norm_w if emit_norm else norm_w)
    return (outs[0], outs[1] if emit_norm else None), converted


IN_PROJ_ROWS = 1024


def _in_proj_body(xn_ref, w_ref, wkr_ref, z_ref, kr_ref):
    @pl.when(pl.program_id(1) == 0)
    def _():
        kr_ref[...] = _dot_nt(xn_ref[...], wkr_ref[...])

    rows_per = min(xn_ref.shape[0], IN_PROJ_ROWS)
    for r in range(xn_ref.shape[0] // rows_per):
        rows = slice(r * rows_per, (r + 1) * rows_per)
        z_ref[rows, :] = _dot_nt(xn_ref[rows, :], w_ref[...]).astype(BF16)


def _in_proj(xn, w_main_t, w_kr_t, *, tm=2048, tn=1024):
    t, d = xn.shape
    n = w_main_t.shape[0]
    return pl.pallas_call(
        _in_proj_body,
        grid=(t // tm, n // tn),
        in_specs=[
            pl.BlockSpec((tm, d), lambda i, j: (i, 0)),
            pl.BlockSpec((tn, d), lambda i, j: (j, 0)),
            pl.BlockSpec((LANES, d), lambda i, j: (0, 0)),
        ],
        out_specs=[
            pl.BlockSpec((tm, tn), lambda i, j: (i, j)),
            pl.BlockSpec((tm, LANES), lambda i, j: (i, 0)),
        ],
        out_shape=[
            jax.ShapeDtypeStruct((t, n), BF16),
            jax.ShapeDtypeStruct((t, LANES), F32),
        ],
        compiler_params=_cparams(("parallel", "arbitrary")),
        name="in_proj",
    )(xn, w_main_t, w_kr_t)


def _rope(r, cos_t, sin_t):
    lane = lax.broadcasted_iota(jnp.int32, r.shape, 1)
    half = MLA_ROPE_DIM // 2
    swapped = jnp.where(lane < half, pltpu.roll(r, LANES - half, 1), pltpu.roll(r, half, 1))
    return r * cos_t + swapped * sin_t


def _mla_prep_body(ql_ref, kvl_ref, kr_ref, qn_ref, kvn_ref, wuq_ref, wukv_ref,
                   cos_ref, sin_ref, q_out, k_out, v_out):
    cq = _rms(ql_ref[...].astype(F32), qn_ref[...]).astype(BF16)
    ckv = _rms(kvl_ref[...].astype(F32), kvn_ref[...]).astype(BF16)
    cos_t = cos_ref[...]
    sin_t = sin_ref[...]
    k_rope = _rope(kr_ref[...], cos_t, sin_t).astype(BF16)
    for h in range(MLA_HEADS):
        lo = h * MLA_HEAD_PAD
        mid = lo + LANES
        hi = lo + MLA_HEAD_PAD
        q = jnp.dot(cq, wuq_ref[:, lo:hi], preferred_element_type=F32)
        kv = jnp.dot(ckv, wukv_ref[:, lo:hi], preferred_element_type=F32)
        q_out[:, lo:mid] = q[:, :LANES].astype(BF16)
        q_out[:, mid:hi] = _rope(q[:, LANES:], cos_t, sin_t).astype(BF16)
        k_out[:, lo:mid] = kv[:, :LANES].astype(BF16)
        k_out[:, mid:hi] = k_rope
        v_out[:, h * MLA_V_DIM:(h + 1) * MLA_V_DIM] = kv[:, LANES:].astype(BF16)


def _mla_prep(z, kr, q_norm, kv_norm, wuq_p, wukv, cos_t, sin_t, *, seq, ql_blk, kvl_blk, tm=512):
    t = z.shape[0]
    nseq = seq // tm
    qw = MLA_HEADS * MLA_HEAD_PAD
    return pl.pallas_call(
        _mla_prep_body,
        grid=(t // tm,),
        in_specs=[
            pl.BlockSpec((tm, MLA_Q_RANK), lambda i: (i, ql_blk)),
            pl.BlockSpec((tm, MLA_KV_RANK), lambda i: (i, kvl_blk)),
            pl.BlockSpec((tm, LANES), lambda i: (i, 0)),
            pl.BlockSpec((1, MLA_Q_RANK), lambda i: (0, 0)),
            pl.BlockSpec((1, MLA_KV_RANK), lambda i: (0, 0)),
            pl.BlockSpec((MLA_Q_RANK, qw), lambda i: (0, 0)),
            pl.BlockSpec((MLA_KV_RANK, qw), lambda i: (0, 0)),
            pl.BlockSpec((tm, LANES), lambda i: (i % nseq, 0)),
            pl.BlockSpec((tm, LANES), lambda i: (i % nseq, 0)),
        ],
        out_specs=[
            pl.BlockSpec((tm, qw), lambda i: (i, 0)),
            pl.BlockSpec((tm, qw), lambda i: (i, 0)),
            pl.BlockSpec((tm, MLA_WIDTH), lambda i: (i, 0)),
        ],
        out_shape=[
            jax.ShapeDtypeStruct((t, qw), BF16),
            jax.ShapeDtypeStruct((t, qw), BF16),
            jax.ShapeDtypeStruct((t, MLA_WIDTH), BF16),
        ],
        compiler_params=_cparams(("parallel",)),
        name="mla_prep",
    )(z, z, kr, q_norm, kv_norm, wuq_p, wukv, cos_t, sin_t)


RPB_ROWS = 2 * NA_WIN_ROWS - 1
RPB_COLS = 2 * NA_WIN_COLS - 1
NA_Q_ROWS = 4
NA_K_ROWS = NA_Q_ROWS + NA_WIN_ROWS
NA_VARIANTS = 3


def _na_variant_geometry(var, j):
    lo = (0, j, NA_K_ROWS - NA_WIN_ROWS)[var]
    rho0 = (NA_WIN_ROWS - 1) - j - (NA_WIN_ROWS // 2) * var
    return lo, rho0


def _na_bias_body(rpb_ref, o_ref, pair_ref):
    h = pl.program_id(0)
    kw = NA_WIN_COLS
    shape = (GRID_W, LANES)
    q = lax.broadcasted_iota(jnp.int32, shape, 0)
    lane = lax.broadcasted_iota(jnp.int32, shape, 1)
    first = lane < GRID_W
    k = jnp.where(first, lane, lane - GRID_W)
    col_start = jnp.clip(q - kw // 2, 0, GRID_W - kw)
    valid = (k >= col_start) & (k < col_start + kw)
    dc = jnp.clip(k - q, -(kw - 1), kw - 1) + (kw - 1)
    neg = jnp.full(shape, NEG_INF, F32)

    for e in range(RPB_ROWS + 1):
        base_a = (h * RPB_ROWS + max(e - 1, 0)) * RPB_COLS
        base_b = (h * RPB_ROWS + min(e, RPB_ROWS - 1)) * RPB_COLS
        acc = jnp.zeros(shape, F32)
        for c in range(RPB_COLS):
            val = jnp.where(first, rpb_ref[base_a + c], rpb_ref[base_b + c])
            acc = jnp.where(dc == c, val, acc)
        pair_ref[e] = jnp.where(valid, acc * LOG2E, NEG_INF)

    for var in range(NA_VARIANTS):
        for j in range(NA_Q_ROWS):
            lo, rho0 = _na_variant_geometry(var, j)
            for pair in range(NA_K_ROWS // 2):
                i = 2 * pair
                in_a = lo <= i < lo + NA_WIN_ROWS
                in_b = lo <= i + 1 < lo + NA_WIN_ROWS
                if in_a or in_b:
                    tile = pair_ref[rho0 + i + 1]
                    if not in_a:
                        tile = jnp.where(first, neg, tile)
                    if not in_b:
                        tile = jnp.where(first, tile, neg)
                else:
                    tile = neg
                o_ref[var, 0, j * GRID_W:(j + 1) * GRID_W, pair * LANES:(pair + 1) * LANES] = tile


def _na_bias(rpb_flat):
    nq = NA_Q_ROWS * GRID_W
    nk = NA_K_ROWS * GRID_W
    return pl.pallas_call(
        _na_bias_body,
        grid=(NA_HEADS,),
        in_specs=[pl.BlockSpec(memory_space=pltpu.SMEM)],
        out_specs=pl.BlockSpec((NA_VARIANTS, 1, nq, nk), lambda h: (0, h, 0, 0)),
        out_shape=jax.ShapeDtypeStruct((NA_VARIANTS, NA_HEADS, nq, nk), F32),
        scratch_shapes=[pltpu.VMEM((RPB_ROWS + 1, GRID_W, LANES), F32)],
        compiler_params=_cparams(("arbitrary",)),
        name="na_bias",
    )(rpb_flat)


def _na_key_start(g, rows):
    return jnp.clip(g * NA_Q_ROWS - NA_WIN_ROWS // 2, 0, rows - NA_K_ROWS)


def _na_kv_copy(z_hbm, kv_buf, sem, batch, which, col_blk):
    slot = batch % 2
    return pltpu.make_async_copy(z_hbm.at[batch, :, pl.ds(col_blk * NA_WIDTH, NA_WIDTH)],
                                 kv_buf.at[slot, which], sem.at[slot, which])


def _na_body(q_ref, z_hbm, b_ref, o_ref, kv_buf, sem, *, rows, nbatch, k_blk, v_blk):
    bi, g = pl.program_id(0), pl.program_id(1)

    def copies(batch):
        return (_na_kv_copy(z_hbm, kv_buf, sem, batch, 0, k_blk),
                _na_kv_copy(z_hbm, kv_buf, sem, batch, 1, v_blk))

    @pl.when((g == 0) & (bi == 0))
    def _():
        for cp in copies(0):
            cp.start()

    @pl.when((g == 0) & (bi + 1 < nbatch))
    def _():
        for cp in copies(bi + 1):
            cp.start()

    @pl.when(g == 0)
    def _():
        for cp in copies(bi):
            cp.wait()

    k_ref = kv_buf.at[bi % 2, 0]
    v_ref = kv_buf.at[bi % 2, 1]
    start = pl.multiple_of(_na_key_start(g, rows) * GRID_W, GRID_W)
    kind = jnp.where(g == 0, 0, jnp.where(g == rows // NA_Q_ROWS - 1, 2, 1))
    nkeys = NA_K_ROWS * GRID_W
    c = (NA_HEAD_DIM ** -0.5) * LOG2E
    ones = jnp.ones((nkeys, NA_HEAD_DIM), BF16)
    for h in range(NA_HEADS):
        cs = slice(h * NA_HEAD_DIM, (h + 1) * NA_HEAD_DIM)
        q = q_ref[0, :, cs]
        k = k_ref[pl.ds(start, nkeys), cs]
        v = v_ref[pl.ds(start, nkeys), cs]
        s = _dot_nt(q, k)
        t = s * c + b_ref[kind, h]
        m = jnp.max(t, axis=-1, keepdims=True)
        p = jnp.exp2(t - m).astype(BF16)
        oe = jnp.dot(p, jnp.concatenate([v, ones], axis=1), preferred_element_type=F32)
        o_ref[0, :, cs] = (oe[:, :NA_HEAD_DIM] / oe[:, NA_HEAD_DIM:]).astype(BF16)


def _na_attn(z3, bias, *, q_blk, k_blk, v_blk):
    b, s, _ = z3.shape
    rows = s // GRID_W
    assert rows % NA_Q_ROWS == 0 and rows >= 2 * NA_K_ROWS - NA_WIN_ROWS
    groups = rows // NA_Q_ROWS
    nq = NA_Q_ROWS * GRID_W
    return pl.pallas_call(
        functools.partial(_na_body, rows=rows, nbatch=b, k_blk=k_blk, v_blk=v_blk),
        grid=(b, groups),
        in_specs=[
            pl.BlockSpec((1, nq, NA_WIDTH), lambda bi, g: (bi, g, q_blk)),
            pl.BlockSpec(memory_space=pl.ANY),
            pl.BlockSpec(bias.shape, lambda bi, g: (0, 0, 0, 0), pipeline_mode=pl.Buffered(1)),
        ],
        out_specs=pl.BlockSpec((1, nq, NA_WIDTH), lambda bi, g: (bi, g, 0)),
        out_shape=jax.ShapeDtypeStruct((b, s, NA_WIDTH), BF16),
        scratch_shapes=[pltpu.VMEM((2, 2, s, NA_WIDTH), BF16), pltpu.SemaphoreType.DMA((2, 2))],
        compiler_params=_cparams(("arbitrary", "arbitrary")),
        name="na_attn",
    )(z3, z3, bias)


def _mla_body(*refs, n_side, tq, tk):
    q_ref, k_ref, v_ref = refs[:3]
    o_ref = refs[3 + n_side]
    _run_side_jobs(refs[3:3 + n_side], refs[4 + n_side:], None, None)
    seq = q_ref.shape[1]
    c = (MLA_QK_DIM ** -0.5) * LOG2E
    ones = jnp.ones((tk, MLA_HEAD_PAD - MLA_V_DIM), BF16)
    for qi in range(seq // tq):
        q = q_ref[0, qi * tq:(qi + 1) * tq, :]
        m = acc = None
        for kj in range(seq // tk):
            ks = slice(kj * tk, (kj + 1) * tk)
            s = _dot_nt(q, k_ref[0, ks, :]) * c
            mj = jnp.max(s, axis=-1, keepdims=True)
            m_new = mj if m is None else jnp.maximum(m, mj)
            p = jnp.exp2(s - m_new).astype(BF16)
            pv = jnp.dot(p, jnp.concatenate([v_ref[0, ks, :], ones], axis=1),
                         preferred_element_type=F32)
            acc = pv if acc is None else acc * jnp.exp2(m - m_new) + pv
            m = m_new
        o_ref[0, qi * tq:(qi + 1) * tq, :] = (acc[:, :MLA_V_DIM] / acc[:, MLA_V_DIM:]).astype(BF16)


def _mla_attn(q3, k3, v3, *, casts=(), tq=512, tk=512):
    b, s, _ = q3.shape
    vw = v3.shape[2] // MLA_HEADS
    side = _side_job_specs(casts, None, None, b * MLA_HEADS, lambda bi, h: bi * MLA_HEADS + h)
    outs = pl.pallas_call(
        functools.partial(_mla_body, n_side=len(side[0]), tq=tq, tk=tk),
        grid=(b, MLA_HEADS),
        in_specs=[
            pl.BlockSpec((1, s, MLA_HEAD_PAD), lambda bi, h: (bi, 0, h)),
            pl.BlockSpec((1, s, MLA_HEAD_PAD), lambda bi, h: (bi, 0, h)),
            pl.BlockSpec((1, s, vw), lambda bi, h: (bi, 0, h)),
        ] + side[0],
        out_specs=[pl.BlockSpec((1, s, MLA_V_DIM), lambda bi, h: (bi, 0, h))] + side[1],
        out_shape=[jax.ShapeDtypeStruct((b, s, MLA_WIDTH), BF16)] + side[2],
        compiler_params=_cparams(("arbitrary", "arbitrary")),
        name="mla_attn",
    )(q3, k3, v3, *side[3])
    return outs[0], list(outs[1:])


def _merge_body(na_ref, mla_ref, ga_ref, gb_ref, h_ref, wa_ref, wb_ref, wo_ref, nw_ref,
                o_ref, xn_ref):
    ya = jnp.dot(na_ref[...], wa_ref[...], preferred_element_type=F32)
    yb = jnp.dot(mla_ref[...], wb_ref[...], preferred_element_type=F32)
    merged = _sigmoid(ga_ref[...].astype(F32)) * ya + _sigmoid(gb_ref[...].astype(F32)) * yb
    h = h_ref[...] + jnp.dot(merged.astype(BF16), wo_ref[...], preferred_element_type=F32)
    o_ref[...] = h
    xn_ref[...] = _rms(h, nw_ref[...]).astype(BF16)


def _merge(na_o, mla_o, z, h, wa, wb, wo, next_norm_w, *, ga_blk, gb_blk, tm=256):
    t, d = h.shape
    return pl.pallas_call(
        _merge_body,
        grid=(t // tm,),
        in_specs=[
            pl.BlockSpec((tm, NA_WIDTH), lambda i: (i, 0)),
            pl.BlockSpec((tm, MLA_WIDTH), lambda i: (i, 0)),
            pl.BlockSpec((tm, d), lambda i: (i, ga_blk)),
            pl.BlockSpec((tm, d), lambda i: (i, gb_blk)),
            pl.BlockSpec((tm, d), lambda i: (i, 0)),
            _resident((NA_WIDTH, d)),
            _resident((MLA_WIDTH, d)),
            _resident((d, d)),
            _resident((1, d)),
        ],
        out_specs=[pl.BlockSpec((tm, d), lambda i: (i, 0))] * 2,
        out_shape=[jax.ShapeDtypeStruct((t, d), F32), jax.ShapeDtypeStruct((t, d), BF16)],
        compiler_params=_cparams(("parallel",)),
        name="merge",
    )(na_o, mla_o, z, z, h, wa, wb, wo, next_norm_w)


def _pl_body(h_ref, p_ref, nw_ref, wg_ref, wp_ref, fn_ref, o_ref, *, final):
    h = h_ref[...]
    xn = _rms(h, nw_ref[...]).astype(BF16)
    gate = _sigmoid(jnp.dot(xn, wg_ref[...], preferred_element_type=F32))
    emb = jnp.dot(p_ref[...].astype(BF16), wp_ref[...], preferred_element_type=F32)
    h = h + gate * emb
    o_ref[...] = _rms(h, fn_ref[...]) if final else h


def _pl_embed(h, p, norm_w, w_gate, w_pl, final_norm, *, final, tm=512):
    t, d = h.shape
    pdim = p.shape[1]
    return pl.pallas_call(
        functools.partial(_pl_body, final=final),
        grid=(t // tm,),
        in_specs=[
            pl.BlockSpec((tm, d), lambda i: (i, 0)),
            pl.BlockSpec((tm, pdim), lambda i: (i, 0)),
            _resident((1, d)),
            _resident((d, d)),
            _resident((pdim, d)),
            _resident((1, d)),
        ],
        out_specs=pl.BlockSpec((tm, d), lambda i: (i, 0)),
        out_shape=jax.ShapeDtypeStruct((t, d), F32),
        compiler_params=_cparams(("parallel",)),
        name="pl_final" if final else "pl_embed",
    )(h, p, norm_w, w_gate, w_pl, final_norm)


def _final_norm_body(h_ref, fn_ref, o_ref):
    o_ref[...] = _rms(h_ref[...], fn_ref[...])


def _rope_tables(seq):
    half = MLA_ROPE_DIM // 2
    pos = jnp.arange(seq, dtype=F32)
    inv_freq = 1.0 / (ROPE_THETA ** (jnp.arange(0, MLA_ROPE_DIM, 2, dtype=F32) / MLA_ROPE_DIM))
    ang = pos[:, None] * inv_freq[None, :]
    cos, sin = jnp.cos(ang), jnp.sin(ang)
    zeros = jnp.zeros((seq, LANES - 2 * half), F32)
    return (jnp.concatenate([cos, cos, zeros], axis=-1),
            jnp.concatenate([-sin, sin, zeros], axis=-1))


def kernel(x, p, ffn1_norm, ffn1_w_gate, ffn1_w_up, ffn1_w_down, mix_norm, w_in, q_a_norm, w_uq, kv_a_norm, w_ukv, na_rpb, w_branch_a, w_branch_b, w_out, ffn2_norm, ffn2_w_gate, ffn2_w_up, ffn2_w_down, pl_norm, w_pl, w_pl_gate, final_norm):
    bsz, seq, d = x.shape
    depth = p.shape[0]
    t = bsz * seq
    bf = lambda w: w.astype(BF16)
    row = lambda g: g.reshape(1, -1).astype(F32)

    c_lat = 3 * NA_WIDTH
    c_rope = c_lat + MLA_Q_RANK + MLA_KV_RANK
    cos_t, sin_t = _rope_tables(seq)

    h = x.reshape(t, d)
    for i in range(depth):
        (h, xn_mix), (w_main_t, w_kr_t) = _ffn(
            h, row(ffn1_norm[i]), ffn1_w_gate[i], ffn1_w_up[i], ffn1_w_down[i],
            next_norm_w=row(mix_norm[i]),
            split_src=jnp.swapaxes(w_in[i], 0, 1), split=(c_rope, MLA_ROPE_DIM))
        z, kr = _in_proj(xn_mix, w_main_t, w_kr_t)

        wuq = w_uq[i].reshape(MLA_Q_RANK, MLA_HEADS, MLA_QK_DIM)
        wuq_p = bf(jnp.pad(wuq, ((0, 0), (0, 0), (0, MLA_HEAD_PAD - MLA_QK_DIM)))
                   .reshape(MLA_Q_RANK, MLA_HEADS * MLA_HEAD_PAD))
        q_m, k_m, v_m = _mla_prep(
            z, kr, row(q_a_norm[i]), row(kv_a_norm[i]), wuq_p, bf(w_ukv[i]), cos_t, sin_t,
            seq=seq, ql_blk=c_lat // MLA_Q_RANK, kvl_blk=(c_lat + MLA_Q_RANK) // MLA_KV_RANK)

        bias = _na_bias(na_rpb[i].reshape(-1).astype(F32))
        z3 = z.reshape(bsz, seq, z.shape[1])
        na_o = _na_attn(z3, bias, q_blk=0, k_blk=1, v_blk=2)
        later = [ffn2_w_gate[i], ffn2_w_up[i], ffn2_w_down[i], w_branch_a[i], w_branch_b[i],
                 w_out[i], w_pl_gate[i]]
        mla_o, (w2_gate, w2_up, w2_down, wa, wb, wo, w_plg) = _mla_attn(
            q_m.reshape(bsz, seq, -1), k_m.reshape(bsz, seq, -1), v_m.reshape(bsz, seq, -1),
            casts=later)

        gate_col = c_rope
        h, xn2 = _merge(na_o.reshape(t, NA_WIDTH), mla_o.reshape(t, MLA_WIDTH), z, h, wa, wb, wo,
                        row(ffn2_norm[i]), ga_blk=gate_col // d, gb_blk=gate_col // d + 1)

        (h, _), _ = _ffn(h, row(ffn2_norm[i]), w2_gate, w2_up, w2_down, xn=xn2)

        h = _pl_embed(h, p[i].reshape(t, -1), row(pl_norm[i]), w_plg, bf(w_pl[i]),
                      row(final_norm), final=(i == depth - 1))

    if depth == 0:
        tm = 512
        h = pl.pallas_call(
            _final_norm_body,
            grid=(t // tm,),
            in_specs=[pl.BlockSpec((tm, d), lambda i: (i, 0)),
                      pl.BlockSpec((1, d), lambda i: (0, 0))],
            out_specs=pl.BlockSpec((tm, d), lambda i: (i, 0)),
            out_shape=jax.ShapeDtypeStruct((t, d), F32),
            compiler_params=_cparams(("parallel",)),
            name="final_norm",
        )(h, row(final_norm))
    return h.reshape(bsz, seq, d)
```
